```python
import math
import jax
import jax.numpy as jnp
from jax import lax
import numpy as np

D_MODEL = 1024
BATCH = 16
SEQ = 256
DEPTH = 4
DEC_BATCH = 2
DEC_SEQ = 4096
PAST_LEN = 256

GRID_W = 64
MIX_W = D_MODEL
GROUP_W = MIX_W // 4
H_A = 4
DV_A = GROUP_W // H_A
DQK_A = DV_A // 2
H_B = 4
DK_B = GROUP_W // H_B
DV_B = GROUP_W // H_B
H_C = 4
NOPE_C = 64
ROPE_C = 32
V_C = GROUP_W // H_C
Q_RANK = 256
KV_RANK = 128
H_D = 4
P_D = GROUP_W // H_D
N_D = 128
G_D = 2
CONV_W = 3
CHUNK = 64
Q_BLOCK = 128
D_FF = 4 * D_MODEL
ROPE_BASE = 10000.0
EPS = 1e-6
ALPHA = (2 * DEPTH) ** 0.25
BETA_INIT = (8 * DEPTH) ** -0.25
SPLIT_SIZES = (
    H_A * 2 * DQK_A, H_A * 2 * DQK_A, H_A * DV_A,
    H_B * (2 * DK_B + DV_B), 2 * H_B, 2 * H_B, H_B * DV_B,
    Q_RANK, KV_RANK, ROPE_C,
    H_D * P_D, H_D * P_D + 2 * G_D * N_D, 2 * H_D,
)
IN_COLS = sum(SPLIT_SIZES)

kernel_name = 'hybrid_diffusion_parallel_heads_step'


def rms_norm(x, g):
    xf = x.astype(jnp.float32)
    y = xf * lax.rsqrt(jnp.mean(xf * xf, axis=-1, keepdims=True) + EPS)
    return y * g.astype(jnp.float32)


def layer_norm(x, g, b):
    xf = x.astype(jnp.float32)
    mu = jnp.mean(xf, axis=-1, keepdims=True)
    var = jnp.mean(jnp.square(xf - mu), axis=-1, keepdims=True)
    y = (xf - mu) * lax.rsqrt(var + EPS) * g.astype(jnp.float32) + b.astype(jnp.float32)
    return y.astype(x.dtype)


def l2_normalize(x):
    xf = x.astype(jnp.float32)
    return xf * lax.rsqrt(jnp.sum(xf * xf, axis=-1, keepdims=True) + EPS)


def depthwise_conv(x, w, b=None):
    pad = CONV_W // 2
    y = lax.conv_general_dilated(x, w[:, None, :].astype(x.dtype), window_strides=(1,),
                                 padding=[(pad, pad)], dimension_numbers=('NWC', 'WIO', 'NWC'),
                                 feature_group_count=x.shape[-1])
    return y if b is None else y + b.astype(x.dtype)


def rope2d_tables(length, dim):
    rows = length // GRID_W
    row_pos = jnp.repeat(jnp.arange(rows, dtype=jnp.float32), GRID_W)
    col_pos = jnp.tile(jnp.arange(GRID_W, dtype=jnp.float32), rows)
    half = dim // 2
    inv_freq = ROPE_BASE ** (-jnp.arange(0, half, 2, dtype=jnp.float32) / half)
    ang_r = row_pos[:, None] * inv_freq
    ang_c = col_pos[:, None] * inv_freq
    ang = jnp.concatenate([ang_r, ang_r, ang_c, ang_c], axis=-1)
    return jnp.cos(ang), jnp.sin(ang)


def apply_rope2d(x, cos, sin):
    qd = x.shape[-1] // 4
    x1, x2, x3, x4 = (x[..., i * qd:(i + 1) * qd] for i in range(4))
    rot = jnp.concatenate([-x2, x1, -x4, x3], axis=-1)
    shape = (1, x.shape[1]) + (1,) * (x.ndim - 3) + (x.shape[-1],)
    return x * cos.reshape(shape).astype(x.dtype) + rot * sin.reshape(shape).astype(x.dtype)


def over_query_blocks(fn, q):
    b, lq = q.shape[:2]
    nb = lq // Q_BLOCK
    blocks = jnp.moveaxis(q.reshape(b, nb, Q_BLOCK, *q.shape[2:]), 1, 0)
    out = jnp.moveaxis(lax.map(fn, blocks), 0, 1)
    return out.reshape(b, lq, *out.shape[3:])


def diff_attention(q, k, v, lam):
    scale = DQK_A ** -0.5

    def block(qb):
        s = jnp.einsum('bqhtd,bkhtd->bhtqk', qb, k).astype(jnp.float32) * scale
        p = jax.nn.softmax(s, axis=-1)
        w = p[:, :, 0] - lam * p[:, :, 1]
        return jnp.einsum('bhqk,bkhv->bqhv', w.astype(v.dtype), v)

    return over_query_blocks(block, q)


def softmax_attention(q, k, v):
    scale = q.shape[-1] ** -0.5

    def block(qb):
        s = jnp.einsum('bqhd,bkhd->bhqk', qb, k).astype(jnp.float32) * scale
        p = jax.nn.softmax(s, axis=-1)
        return jnp.einsum('bhqk,bkhv->bqhv', p.astype(v.dtype), v)

    return over_query_blocks(block, q)


def to_chunks(t):
    b, l, h = t.shape[:3]
    t = t.reshape(b, l // CHUNK, CHUNK, h, *t.shape[3:])
    return jnp.moveaxis(t, 3, 2)


def from_chunks(t):
    t = jnp.moveaxis(t, 2, 3)
    return t.reshape(t.shape[0], t.shape[1] * t.shape[2], t.shape[3], t.shape[4])


def gated_delta_chunked(q, k, v, beta, g, s0):
    qc, kc, vc, bc = to_chunks(q), to_chunks(k), to_chunks(v), to_chunks(beta)
    gc = jnp.cumsum(to_chunks(g), axis=-1)
    idx = jnp.arange(CHUNK)
    causal = idx[:, None] >= idx[None, :]
    strict = idx[:, None] > idx[None, :]
    decay = jnp.exp(jnp.where(causal, gc[..., :, None] - gc[..., None, :], -jnp.inf))
    kb = kc * bc[..., None]
    a_mat = jnp.where(strict, jnp.einsum('bchid,bchjd->bchij', kb, kc) * decay, 0.0)
    eye = jnp.eye(CHUNK, dtype=jnp.float32)
    t_mat = lax.linalg.triangular_solve(eye + a_mat, jnp.broadcast_to(eye, a_mat.shape),
                                        left_side=True, lower=True, unit_diagonal=True)
    u = t_mat @ (vc * bc[..., None])
    w = t_mat @ (kb * jnp.exp(gc)[..., None])
    qk = jnp.where(causal, jnp.einsum('bchid,bchjd->bchij', qc, kc) * decay, 0.0)
    q_dec = qc * jnp.exp(gc)[..., None]
    k_dec = kc * jnp.exp(gc[..., -1:] - gc)[..., None]
    g_last = gc[..., -1]

    def step(s, inp):
        q_i, k_i, u_i, w_i, qk_i, gl_i = inp
        v_new = u_i - w_i @ s
        o = q_i @ s + qk_i @ v_new
        s = s * jnp.exp(gl_i)[..., None, None] + jnp.einsum('bhck,bhcv->bhkv', k_i, v_new)
        return s, o

    xs = tuple(jnp.moveaxis(t, 1, 0) for t in (q_dec, k_dec, u, w, qk, g_last))
    s_fin, o = lax.scan(step, s0, xs)
    return from_chunks(jnp.moveaxis(o, 0, 1)), s_fin


def ssd_chunked(x, dt, a, bm, cm, s0):
    rep = x.shape[2] // bm.shape[2]
    bc = to_chunks(jnp.repeat(bm, rep, axis=2))
    cc = to_chunks(jnp.repeat(cm, rep, axis=2))
    xdt = to_chunks(x * dt[..., None])
    ac = jnp.cumsum(to_chunks(dt * a), axis=-1)
    idx = jnp.arange(CHUNK)
    causal = idx[:, None] >= idx[None, :]
    seg = jnp.exp(jnp.where(causal, ac[..., :, None] - ac[..., None, :], -jnp.inf))
    y_intra = jnp.einsum('bchij,bchjp->bchip', jnp.einsum('bchis,bchjs->bchij', cc, bc) * seg, xdt)
    last = ac[..., -1]
    states = jnp.einsum('bchjs,bchjp->bchps', bc * jnp.exp(last[..., None] - ac)[..., None], xdt)
    c_dec = cc * jnp.exp(ac)[..., None]

    def step(s, inp):
        c_i, st_i, last_i = inp
        y = jnp.einsum('bhis,bhps->bhip', c_i, s)
        s = s * jnp.exp(last_i)[..., None, None] + st_i
        return s, y

    xs = tuple(jnp.moveaxis(t, 1, 0) for t in (c_dec, states, last))
    s_fin, y_inter = lax.scan(step, s0, xs)
    return from_chunks(y_intra + jnp.moveaxis(y_inter, 0, 1)), s_fin


def mixer_diff(q, k, v, lam_p, norm_g, layer, rope, past_k, past_v):
    b, l = q.shape[:2]
    q = q.reshape(b, l, H_A, 2, DQK_A)
    k = k.reshape(b, l, H_A, 2, DQK_A)
    v = v.reshape(b, l, H_A, DV_A)
    own_k, own_v = k, v
    if rope is not None:
        q = apply_rope2d(q, *rope)
        k = apply_rope2d(k, *rope)
    if past_k is not None:
        k = jnp.concatenate([k, past_k.astype(k.dtype)], axis=1)
        v = jnp.concatenate([v, past_v.astype(v.dtype)], axis=1)
    lam_init = 0.8 - 0.6 * math.exp(-0.3 * layer)
    lp = lam_p.astype(jnp.float32)
    lam = jnp.exp(jnp.sum(lp[0] * lp[1])) - jnp.exp(jnp.sum(lp[2] * lp[3])) + lam_init
    o = diff_attention(q, k, v, lam)
    o = rms_norm(o, norm_g) * (1.0 - lam_init)
    return o.reshape(b, l, GROUP_W).astype(q.dtype), own_k, own_v


def mixer_deltanet(qkv, beta_raw, decay_raw, gate, conv_w, a_log, dt_bias, norm_g, past_state):
    b, l = qkv.shape[:2]
    qkv = jax.nn.silu(depthwise_conv(qkv, conv_w))
    q, k, v = jnp.split(qkv, [H_B * DK_B, 2 * H_B * DK_B], axis=-1)
    q = l2_normalize(q.reshape(b, l, H_B, DK_B)) * DK_B ** -0.5
    k = l2_normalize(k.reshape(b, l, H_B, DK_B))
    v = v.reshape(b, l, H_B, DV_B).astype(jnp.float32)
    beta = jax.nn.sigmoid(beta_raw.astype(jnp.float32)).reshape(b, l, 2, H_B)
    g = -jnp.exp(a_log.astype(jnp.float32)) * jax.nn.softplus(
        decay_raw.astype(jnp.float32).reshape(b, l, 2, H_B) + dt_bias.astype(jnp.float32))
    if past_state is None:
        s0 = jnp.zeros((b, 2, H_B, DK_B, DV_B), jnp.float32)
    else:
        s0 = past_state.astype(jnp.float32)
    fl = lambda t: jnp.flip(t, axis=1)
    o_f, s_f = gated_delta_chunked(q, k, v, beta[:, :, 0], g[:, :, 0], s0[:, 0])
    o_b, s_b = gated_delta_chunked(fl(q), fl(k), fl(v), fl(beta[:, :, 1]), fl(g[:, :, 1]), s0[:, 1])
    o = rms_norm(o_f + fl(o_b), norm_g) * jax.nn.silu(gate.reshape(b, l, H_B, DV_B).astype(jnp.float32))
    return o.reshape(b, l, GROUP_W).astype(qkv.dtype), jnp.stack([s_f, s_b], axis=1)


def mixer_mla(c_q, c_kv, k_rope, q_norm, kv_norm, w_uq, w_uk, w_uv, rope, past_ckv, past_kr):
    b, l = c_q.shape[:2]
    q = (rms_norm(c_q, q_norm).astype(c_q.dtype) @ w_uq).reshape(b, l, H_C, NOPE_C + ROPE_C)
    q_nope, q_rope = q[..., :NOPE_C], q[..., NOPE_C:]
    ckv = rms_norm(c_kv, kv_norm).astype(c_kv.dtype)
    own_ckv, own_kr = ckv, k_rope
    if rope is not None:
        q_rope = apply_rope2d(q_rope, *rope)
        k_rope = apply_rope2d(k_rope, *rope)
    if past_ckv is not None:
        ckv = jnp.concatenate([ckv, past_ckv.astype(ckv.dtype)], axis=1)
        k_rope = jnp.concatenate([k_rope, past_kr.astype(k_rope.dtype)], axis=1)
    lk = ckv.shape[1]
    k_nope = (ckv @ w_uk).reshape(b, lk, H_C, NOPE_C)
    v = (ckv @ w_uv).reshape(b, lk, H_C, V_C)
    k = jnp.concatenate([k_nope, jnp.broadcast_to(k_rope[:, :, None, :], (b, lk, H_C, ROPE_C))], axis=-1)
    q = jnp.concatenate([q_nope, q_rope], axis=-1)
    o = softmax_attention(q, k, v)
    return o.reshape(b, l, GROUP_W), own_ckv, own_kr


def mixer_ssd(z, xbc, dt_raw, conv_w, conv_b, a_log, dt_bias, d_skip, norm_g, past_state):
    b, l = z.shape[:2]
    xbc = jax.nn.silu(depthwise_conv(xbc, conv_w, conv_b))
    x, bm, cm = jnp.split(xbc, [H_D * P_D, H_D * P_D + G_D * N_D], axis=-1)
    x = x.reshape(b, l, H_D, P_D).astype(jnp.float32)
    bm = bm.reshape(b, l, G_D, N_D).astype(jnp.float32)
    cm = cm.reshape(b, l, G_D, N_D).astype(jnp.float32)
    dt = jax.nn.softplus(dt_raw.astype(jnp.float32).reshape(b, l, 2, H_D) + dt_bias.astype(jnp.float32))
    a = -jnp.exp(a_log.astype(jnp.float32))
    if past_state is None:
        s0 = jnp.zeros((b, 2, H_D, P_D, N_D), jnp.float32)
    else:
        s0 = past_state.astype(jnp.float32)
    fl = lambda t: jnp.flip(t, axis=1)
    y_f, s_f = ssd_chunked(x, dt[:, :, 0], a[0], bm, cm, s0[:, 0])
    y_b, s_b = ssd_chunked(fl(x), fl(dt[:, :, 1]), a[1], fl(bm), fl(cm), s0[:, 1])
    y = y_f + fl(y_b) + d_skip.astype(jnp.float32)[:, None] * x
    y = y * jax.nn.silu(z.reshape(b, l, H_D, P_D).astype(jnp.float32))
    y = rms_norm(y.reshape(b, l, G_D, GROUP_W // G_D), norm_g.reshape(G_D, GROUP_W // G_D))
    return y.reshape(b, l, GROUP_W).astype(z.dtype), jnp.stack([s_f, s_b], axis=1)


def trunk_layer(x, mod, lw, layer, rope, past):
    shift1, scale1, gate1, shift2, scale2, gate2 = jnp.split(mod[:, None, :].astype(x.dtype), 6, axis=-1)
    h = x * (1 + scale1) + shift1
    parts = jnp.split(h @ lw['w_in'], np.cumsum(SPLIT_SIZES)[:-1].tolist(), axis=-1)
    (a_q, a_k, a_v, b_qkv, b_beta, b_decay, b_gate, c_q, c_kv, c_kr, d_z, d_xbc, d_dt) = parts
    rope_a, rope_c = (None, None) if rope is None else rope
    p = (None,) * 6 if past is None else past
    o_a, ctx_k, ctx_v = mixer_diff(a_q, a_k, a_v, lw['diff_lam'], lw['diff_norm'], layer, rope_a, p[0], p[1])
    o_b, st_b = mixer_deltanet(b_qkv, b_beta, b_decay, b_gate, lw['dn_conv'], lw['dn_a_log'],
                               lw['dn_dt_bias'], lw['dn_norm'], p[2])
    o_c, ctx_ckv, ctx_kr = mixer_mla(c_q, c_kv, c_kr, lw['mla_q_norm'], lw['mla_kv_norm'], lw['mla_w_uq'],
                                     lw['mla_w_uk'], lw['mla_w_uv'], rope_c, p[3], p[4])
    o_d, st_d = mixer_ssd(d_z, d_xbc, d_dt, lw['ssm_conv_w'], lw['ssm_conv_b'], lw['ssm_a_log'],
                          lw['ssm_dt_bias'], lw['ssm_d'], lw['ssm_norm'], p[5])
    mixed = jnp.concatenate([o_a, o_b, o_c, o_d], axis=-1).astype(x.dtype) @ lw['w_out']
    x = layer_norm(ALPHA * x + gate1 * mixed, lw['ln1_g'], lw['ln1_b'])
    h = x * (1 + scale2) + shift2
    ff = jnp.square(jax.nn.relu(h @ lw['w_ff1'])) @ lw['w_ff2']
    x = layer_norm(ALPHA * x + gate2 * ff, lw['ln2_g'], lw['ln2_b'])
    return x, (ctx_k, ctx_v, st_b, ctx_ckv, ctx_kr, st_d)


def _dt_bias(key, shape):
    dt = jnp.exp(jax.random.uniform(key, shape, jnp.float32, math.log(1e-3), math.log(1e-1)))
    return dt + jnp.log(-jnp.expm1(-dt))


def setup_inputs(seed: int = 0) -> dict:
    key = jax.random.key(seed)
    ks = iter(jax.random.split(key, 40))
    f32 = jnp.float32

    def nrm(shape, s):
        return jax.random.normal(next(ks), shape, f32) * s

    return {
        'x_prompt': nrm((BATCH, SEQ, D_MODEL), 1.0),
        'x_sample': nrm((DEC_BATCH, DEC_SEQ, D_MODEL), 1.0),
        'cache_diff_k': nrm((DEC_BATCH, DEPTH, PAST_LEN, H_A, 2, DQK_A), 1.0),
        'cache_diff_v': nrm((DEC_BATCH, DEPTH, PAST_LEN, H_A, DV_A), 1.0),
        'state_delta': nrm((DEC_BATCH, DEPTH, 2, H_B, DK_B, DV_B), 0.1),
        'cache_mla_ckv': nrm((DEC_BATCH, DEPTH, PAST_LEN, KV_RANK), 1.0),
        'cache_mla_krope': nrm((DEC_BATCH, DEPTH, PAST_LEN, ROPE_C), 1.0),
        'state_ssm': nrm((DEC_BATCH, DEPTH, 2, H_D, P_D, N_D), 0.1),
        'c': nrm((DEC_BATCH, D_MODEL), 1.0),
        'c_ctx': nrm((D_MODEL,), 1.0),
        'w_mod': nrm((DEPTH, D_MODEL, 6 * D_MODEL), D_MODEL ** -0.5),
        'b_mod': nrm((DEPTH, 6 * D_MODEL), 0.02),
        'w_in': nrm((DEPTH, D_MODEL, IN_COLS), D_MODEL ** -0.5),
        'diff_lam': nrm((DEPTH, 4, DQK_A), 0.1),
        'diff_norm': 1.0 + nrm((DEPTH, DV_A), 0.02),
        'dn_conv': nrm((DEPTH, CONV_W, H_B * (2 * DK_B + DV_B)), CONV_W ** -0.5),
        'dn_a_log': jnp.log(jax.random.uniform(next(ks), (DEPTH, 2, H_B), f32, 1.0, 16.0)),
        'dn_dt_bias': _dt_bias(next(ks), (DEPTH, 2, H_B)),
        'dn_norm': 1.0 + nrm((DEPTH, DV_B), 0.02),
        'mla_q_norm': 1.0 + nrm((DEPTH, Q_RANK), 0.02),
        'mla_kv_norm': 1.0 + nrm((DEPTH, KV_RANK), 0.02),
        'mla_w_uq': nrm((DEPTH, Q_RANK, H_C * (NOPE_C + ROPE_C)), Q_RANK ** -0.5),
        'mla_w_uk': nrm((DEPTH, KV_RANK, H_C * NOPE_C), KV_RANK ** -0.5),
        'mla_w_uv': nrm((DEPTH, KV_RANK, H_C * V_C), KV_RANK ** -0.5),
        'ssm_conv_w': nrm((DEPTH, CONV_W, H_D * P_D + 2 * G_D * N_D), CONV_W ** -0.5),
        'ssm_conv_b': nrm((DEPTH, H_D * P_D + 2 * G_D * N_D), 0.02),
        'ssm_a_log': jnp.log(jax.random.uniform(next(ks), (DEPTH, 2, H_D), f32, 1.0, 16.0)),
        'ssm_dt_bias': _dt_bias(next(ks), (DEPTH, 2, H_D)),
        'ssm_d': 1.0 + nrm((DEPTH, H_D), 0.1),
        'ssm_norm': 1.0 + nrm((DEPTH, GROUP_W), 0.02),
        'w_out': nrm((DEPTH, MIX_W, D_MODEL), MIX_W ** -0.5 * BETA_INIT),
        'ln1_g': 1.0 + nrm((DEPTH, D_MODEL), 0.02),
        'ln1_b': nrm((DEPTH, D_MODEL), 0.02),
        'ln2_g': 1.0 + nrm((DEPTH, D_MODEL), 0.02),
        'ln2_b': nrm((DEPTH, D_MODEL), 0.02),
        'w_ff1': nrm((DEPTH, D_MODEL, D_FF), D_MODEL ** -0.5),
        'w_ff2': nrm((DEPTH, D_FF, D_MODEL), D_FF ** -0.5 * BETA_INIT),
    }


def reference(x_prompt, x_sample, cache_diff_k, cache_diff_v, state_delta, cache_mla_ckv, cache_mla_krope,
              state_ssm, c, c_ctx, w_mod, b_mod, w_in, diff_lam, diff_norm, dn_conv, dn_a_log, dn_dt_bias,
              dn_norm, mla_q_norm, mla_kv_norm, mla_w_uq, mla_w_uk, mla_w_uv, ssm_conv_w, ssm_conv_b,
              ssm_a_log, ssm_dt_bias, ssm_d, ssm_norm, w_out, ln1_g, ln1_b, ln2_g, ln2_b, w_ff1, w_ff2):
    lat_len = x_sample.shape[1]
    rope = (rope2d_tables(lat_len, DQK_A), rope2d_tables(lat_len, ROPE_C))
    y_prompt, y_sample = x_prompt, x_sample
    ctx_out = ([], [], [], [], [], [])
    for l in range(DEPTH):
        lw = {
            'w_in': w_in[l], 'diff_lam': diff_lam[l], 'diff_norm': diff_norm[l],
            'dn_conv': dn_conv[l], 'dn_a_log': dn_a_log[l], 'dn_dt_bias': dn_dt_bias[l], 'dn_norm': dn_norm[l],
            'mla_q_norm': mla_q_norm[l], 'mla_kv_norm': mla_kv_norm[l], 'mla_w_uq': mla_w_uq[l],
            'mla_w_uk': mla_w_uk[l], 'mla_w_uv': mla_w_uv[l],
            'ssm_conv_w': ssm_conv_w[l], 'ssm_conv_b': ssm_conv_b[l], 'ssm_a_log': ssm_a_log[l],
            'ssm_dt_bias': ssm_dt_bias[l], 'ssm_d': ssm_d[l], 'ssm_norm': ssm_norm[l],
            'w_out': w_out[l], 'ln1_g': ln1_g[l], 'ln1_b': ln1_b[l], 'ln2_g': ln2_g[l], 'ln2_b': ln2_b[l],
            'w_ff1': w_ff1[l], 'w_ff2': w_ff2[l],
        }
        mod_ctx = jax.nn.silu(c_ctx)[None, :] @ w_mod[l] + b_mod[l]
        mod_lat = jax.nn.silu(c) @ w_mod[l] + b_mod[l]
        y_prompt, ctx_t = trunk_layer(y_prompt, mod_ctx, lw, l, None, None)
        past = (cache_diff_k[:, l], cache_diff_v[:, l], state_delta[:, l],
                cache_mla_ckv[:, l], cache_mla_krope[:, l], state_ssm[:, l])
        y_sample, _ = trunk_layer(y_sample, mod_lat, lw, l, rope, past)
        for store, t in zip(ctx_out, ctx_t):
            store.append(t)
    new_diff_k = jnp.stack(ctx_out[0], axis=1)
    new_diff_v = jnp.stack(ctx_out[1], axis=1)
    new_state_delta = jnp.stack(ctx_out[2], axis=1)
    new_mla_ckv = jnp.stack(ctx_out[3], axis=1)
    new_mla_krope = jnp.stack(ctx_out[4], axis=1)
    new_state_ssm = jnp.stack(ctx_out[5], axis=1)
    return (y_prompt, y_sample, new_diff_k, new_diff_v, new_state_delta, new_mla_ckv, new_mla_krope, new_state_ssm)
```

```python
import functools
import math

import jax
import jax.numpy as jnp
from jax import lax
from jax.experimental import pallas as pl
from jax.experimental.pallas import tpu as pltpu

F32 = jnp.float32
BF16 = jnp.bfloat16

D_MODEL = 1024
DEPTH = 4
GRID_W = 64
GROUP_W = 256
H_A, DV_A, DQK_A = 4, 64, 32
H_B, DK_B, DV_B = 4, 64, 64
H_C, NOPE_C, ROPE_C, V_C = 4, 64, 32, 64
Q_RANK, KV_RANK = 256, 128
H_D, P_D, N_D, G_D = 4, 64, 128, 2
CHUNK = 64
D_FF = 4 * D_MODEL
ROPE_BASE = 10000.0
EPS = 1e-6
ALPHA = (2 * DEPTH) ** 0.25
LOG2E = 1.4426950408889634

SPLIT_NAMES = ("a_q", "a_k", "a_v", "b_qkv", "b_beta", "b_decay", "b_gate",
               "c_q", "c_kv", "c_kr", "d_z", "d_xbc", "d_dt")
SPLIT_SIZES = (256, 256, 256, 768, 8, 8, 256, 256, 128, 32, 256, 768, 8)
OUT_GROUPS = (("a_q", "a_k", "a_v"), ("b_qkv",), ("b_gate",), ("c_q",), ("d_z",), ("d_xbc",),
              ("c_kv",), ("c_kr", "b_beta", "b_decay", "d_dt"))
OUT_WIDTHS = (768, 768, 256, 256, 256, 768, 128, 128)
IN_COLS_PAD = sum(OUT_WIDTHS)
COL_KR, COL_BETA, COL_DECAY, COL_DT = 0, 32, 40, 48

TILE = 256
TQ = 128
TB = 512
VMEM_LIMIT = 56 * 1024 * 1024


def _cparams(n_axes):
    return pltpu.CompilerParams(dimension_semantics=("arbitrary",) * n_axes,
                                vmem_limit_bytes=VMEM_LIMIT)


def _dot(a, b):
    return jnp.dot(a, b, preferred_element_type=F32)


def _dot_nt(a, b):
    return lax.dot_general(a, b, (((1,), (1,)), ((), ())), preferred_element_type=F32)


def _split2(x):
    hi = x.astype(BF16)
    lo = (x - hi.astype(F32)).astype(BF16)
    return hi, lo


def _split3(x):
    x1 = x.astype(BF16)
    r = x - x1.astype(F32)
    x2 = r.astype(BF16)
    x3 = (r - x2.astype(F32)).astype(BF16)
    return x1, x2, x3


def _dot3(a, b):
    a1, a2 = _split2(a)
    b1, b2 = _split2(b)
    return _dot(a1, b1) + (_dot(a1, b2) + _dot(a2, b1))


def _sigmoid(x):
    return 1.0 / (1.0 + jnp.exp(-x))


def _silu(x):
    return x * _sigmoid(x)


def _softplus(x):
    return jnp.maximum(x, 0.0) + jnp.log1p(jnp.exp(-jnp.abs(x)))


def _rms_groups(x, width):
    outs = []
    for s in range(0, x.shape[1], width):
        xg = x[:, s:s + width]
        ms = jnp.mean(xg * xg, axis=-1, keepdims=True)
        outs.append(xg * lax.rsqrt(ms + EPS))
    return outs[0] if len(outs) == 1 else jnp.concatenate(outs, axis=1)


def _layer_norm(x, g, b):
    mu = jnp.mean(x, axis=-1, keepdims=True)
    xc = x - mu
    var = jnp.mean(xc * xc, axis=-1, keepdims=True)
    return xc * lax.rsqrt(var + EPS) * g + b


def _rope(x, cos, sin_lo, sin_hi):
    w = x.shape[1]
    return x * cos + pltpu.roll(x, w - 8, 1) * sin_lo + pltpu.roll(x, 8, 1) * sin_hi


def _mod_kernel(c_ref, w_ref, b_ref, o_ref):
    c = c_ref[...]
    o_ref[...] = _dot(_silu(c).astype(BF16), w_ref[...].astype(BF16)) + b_ref[...]


def _mod_call(cvec, w_mod, b_mod):
    nblk = 4
    wcol = 6 * D_MODEL // nblk
    return pl.pallas_call(
        _mod_kernel,
        grid=(DEPTH, nblk),
        in_specs=[pl.BlockSpec((8, D_MODEL), lambda l, j: (0, 0)),
                  pl.BlockSpec((None, D_MODEL, wcol), lambda l, j: (l, 0, j)),
                  pl.BlockSpec((None, 1, wcol), lambda l, j: (l, 0, j))],
        out_specs=pl.BlockSpec((None, 8, wcol), lambda l, j: (l, 0, j)),
        out_shape=jax.ShapeDtypeStruct((DEPTH, 8, 6 * D_MODEL), F32),
        compiler_params=_cparams(2),
        name="mod",
    )(cvec, w_mod, b_mod.reshape(DEPTH, 1, 6 * D_MODEL))


def _inproj_kernel(has_rope, *refs):
    if has_rope:
        (x_ref, mod_ref, w_ref, cos_ref, slo_ref, shi_ref, cos_s_ref, slo_s_ref, shi_s_ref,
         oa, obq, obg, ocq, odz, odx, ockv, osm) = refs
    else:
        x_ref, mod_ref, w_ref, oa, obq, obg, ocq, odz, odx, ockv, osm = refs
    x = x_ref[...]
    h = (x * (1.0 + mod_ref[1:2, :]) + mod_ref[0:1, :]).astype(BF16)
    outs = (oa, obq, obg, ocq, odz, odx, ockv, osm)
    off = 0
    for idx, (o_ref, wd) in enumerate(zip(outs, OUT_WIDTHS)):
        y = _dot(h, w_ref[:, off:off + wd])
        off += wd
        if has_rope and idx == 0:
            cos, slo, shi = cos_ref[...], slo_ref[...], shi_ref[...]
            o_ref[:, 0:256] = _rope(y[:, 0:256], cos, slo, shi)
            o_ref[:, 256:512] = _rope(y[:, 256:512], cos, slo, shi)
            o_ref[:, 512:768] = y[:, 512:768]
        elif has_rope and idx == 7:
            o_ref[...] = _rope(y, cos_s_ref[...], slo_s_ref[...], shi_s_ref[...])
        else:
            o_ref[...] = y


def _inproj(x, mod, w_in_p, seq_len, per_batch_mod, rope_tabs):
    t = x.shape[0]
    nb_seq = seq_len // TB
    has_rope = rope_tabs is not None
    mod_idx = (lambda i: (1 + i // nb_seq, 0, 0)) if per_batch_mod else (lambda i: (0, 0, 0))
    in_specs = [pl.BlockSpec((TB, D_MODEL), lambda i: (i, 0)),
                pl.BlockSpec((None, 6, D_MODEL), mod_idx),
                pl.BlockSpec((D_MODEL, IN_COLS_PAD), lambda i: (0, 0))]
    args = [x, mod, w_in_p]
    if has_rope:
        for tab in rope_tabs:
            in_specs.append(pl.BlockSpec((TB, tab.shape[1]), lambda i: (i % nb_seq, 0)))
            args.append(tab)
    return pl.pallas_call(
        functools.partial(_inproj_kernel, has_rope),
        grid=(t // TB,),
        in_specs=in_specs,
        out_specs=[pl.BlockSpec((TB, wd), lambda i: (i, 0)) for wd in OUT_WIDTHS],
        out_shape=[jax.ShapeDtypeStruct((t, wd), F32) for wd in OUT_WIDTHS],
        compiler_params=_cparams(1),
        name="inproj",
    )(*args)


def _softmax_pv(q16, kt_ref, k_row0, v_ref, v_col0, s_scr, nch):
    dk = q16.shape[1]

    def scores(c, m_run):
        s = _dot(q16, kt_ref[c, k_row0:k_row0 + dk, :])
        s_scr[c] = s
        return jnp.maximum(m_run, jnp.maximum(s[:, :128], s[:, 128:]))

    m_run = lax.fori_loop(0, nch, scores, jnp.full((TQ, 128), -jnp.inf, F32))
    m = jnp.max(m_run, axis=-1, keepdims=True)

    def accum(c, carry):
        l_run, acc = carry
        p = jnp.exp2(s_scr[c] - m)
        l_run = l_run + (p[:, :128] + p[:, 128:])
        acc = acc + _dot(p.astype(BF16), v_ref[c, :, v_col0:v_col0 + 128])
        return l_run, acc

    l_run, acc = lax.fori_loop(0, nch, accum, (jnp.zeros((TQ, 128), F32), jnp.zeros((TQ, 128), F32)))
    return acc, jnp.sum(l_run, axis=-1, keepdims=True)


def _diff_attn_kernel(has_past, nch_own, lam_init, *refs):
    if has_past:
        q_ref, k_ref, v_ref, pk_ref, pv_ref, lam_ref, g_ref, o_ref, kt_scr, v_scr, s_scr = refs
    else:
        q_ref, k_ref, v_ref, lam_ref, g_ref, o_ref, kt_scr, v_scr, s_scr = refs
    nch = nch_own + (1 if has_past else 0)

    @pl.when(pl.program_id(1) == 0)
    def _prep():
        def body(c, carry):
            r0 = pl.multiple_of(c * TILE, TILE)
            kt_scr[c] = k_ref[pl.ds(r0, TILE), :].T.astype(BF16)
            v_scr[c] = v_ref[pl.ds(r0, TILE), :].astype(BF16)
            return carry

        lax.fori_loop(0, nch_own, body, 0)
        if has_past:
            kt_scr[nch_own] = pk_ref[...].T.astype(BF16)
            v_scr[nch_own] = pv_ref[...].astype(BF16)

    lp = lam_ref[...]
    lam = (jnp.exp(jnp.sum(lp[0:1] * lp[1:2], axis=-1, keepdims=True))
           - jnp.exp(jnp.sum(lp[2:3] * lp[3:4], axis=-1, keepdims=True)) + lam_init)
    q = q_ref[...] * (DQK_A ** -0.5 * LOG2E)
    heads = []
    for h in range(H_A):
        parts = []
        for t in range(2):
            c0 = h * 2 * DQK_A + t * DQK_A
            acc, l = _softmax_pv(q[:, c0:c0 + DQK_A].astype(BF16), kt_scr, c0, v_scr, (h // 2) * 128, s_scr, nch)
            parts.append(acc[:, (h % 2) * DV_A:(h % 2 + 1) * DV_A] / l)
        heads.append(parts[0] - lam * parts[1])
    o = jnp.concatenate(heads, axis=1)
    o_ref[...] = _rms_groups(o, DV_A) * g_ref[...] * (1.0 - lam_init)


def _diff_attn(a_qkv, batch, seq_len, layer, lam_p, g_row, past):
    nq = seq_len // TQ
    nch_own = seq_len // TILE
    has_past = past is not None
    nch = nch_own + (1 if has_past else 0)
    lam_init = 0.8 - 0.6 * math.exp(-0.3 * layer)
    in_specs = [pl.BlockSpec((TQ, 256), lambda b, i: (b * nq + i, 0)),
                pl.BlockSpec((seq_len, 256), lambda b, i: (b, 1)),
                pl.BlockSpec((seq_len, 256), lambda b, i: (b, 2))]
    args = [a_qkv, a_qkv, a_qkv]
    if has_past:
        in_specs += [pl.BlockSpec((None, TILE, 256), lambda b, i: (b, 0, 0))] * 2
        args += list(past)
    in_specs += [pl.BlockSpec((4, DQK_A), lambda b, i: (0, 0)), pl.BlockSpec((1, 256), lambda b, i: (0, 0))]
    args += [lam_p, g_row]
    return pl.pallas_call(
        functools.partial(_diff_attn_kernel, has_past, nch_own, lam_init),
        grid=(batch, nq),
        in_specs=in_specs,
        out_specs=pl.BlockSpec((TQ, 256), lambda b, i: (b * nq + i, 0)),
        out_shape=jax.ShapeDtypeStruct((batch * seq_len, 256), F32),
        scratch_shapes=[pltpu.VMEM((nch, 256, TILE), BF16), pltpu.VMEM((nch, TILE, 256), BF16),
                        pltpu.VMEM((nch, TQ, TILE), F32)],
        compiler_params=_cparams(2),
        name="diff_attn",
    )(*args)


def _mla_kernel(has_past, has_rope, nch_own, *refs):
    refs = list(refs)
    cq_ref, ckv_ref, sm_ref = refs[:3]
    pos = 3
    if has_past:
        pckv_ref, pkr_ref = refs[pos:pos + 2]
        pos += 2
    qn_ref, kvn_ref, wuq_ref, wuk_ref, wuv_ref = refs[pos:pos + 5]
    pos += 5
    if has_rope:
        cos_ref, slo_ref, shi_ref = refs[pos:pos + 3]
        pos += 3
    o_ref, ckv_out = refs[pos:pos + 2]
    kt_scr, v_scr, s_scr = refs[pos + 2:]
    nch = nch_own + (1 if has_past else 0)

    def put_chunk(c, ckv, kr):
        c16 = ckv.astype(BF16)
        knt = _dot(c16, wuk_ref[...]).T
        v_scr[c] = _dot(c16, wuv_ref[...]).astype(BF16)
        lane = lax.broadcasted_iota(jnp.int32, kr.shape, 1)
        krt = jnp.where(lane < ROPE_C, kr, 0.0).T[0:64, :].astype(BF16)
        for h in range(H_C):
            kt_scr[c, h * 128:h * 128 + NOPE_C, :] = knt[h * NOPE_C:(h + 1) * NOPE_C, :].astype(BF16)
            kt_scr[c, h * 128 + NOPE_C:(h + 1) * 128, :] = krt

    @pl.when(pl.program_id(1) == 0)
    def _prep():
        def body(c, carry):
            r0 = pl.multiple_of(c * TILE, TILE)
            ckv = _rms_groups(ckv_ref[pl.ds(r0, TILE), :], KV_RANK) * kvn_ref[...]
            ckv_out[pl.ds(r0, TILE), :] = ckv
            put_chunk(c, ckv, sm_ref[pl.ds(r0, TILE), :])
            return carry

        lax.fori_loop(0, nch_own, body, 0)
        if has_past:
            put_chunk(nch_own, pckv_ref[...], pkr_ref[...])

    cqn = (_rms_groups(cq_ref[...], Q_RANK) * qn_ref[...]).astype(BF16)
    qa = _dot(cqn, wuq_ref[...])
    q_nope = qa[:, 0:256]
    q_rope = qa[:, 256:384]
    if has_rope:
        q_rope = _rope(q_rope, cos_ref[...], slo_ref[...], shi_ref[...])
    scale = (NOPE_C + ROPE_C) ** -0.5 * LOG2E
    zpad = jnp.zeros((TQ, 128 - NOPE_C - ROPE_C), F32)
    heads = []
    for h in range(H_C):
        qh = jnp.concatenate([q_nope[:, h * NOPE_C:(h + 1) * NOPE_C],
                              q_rope[:, h * ROPE_C:(h + 1) * ROPE_C], zpad], axis=1) * scale
        acc, l = _softmax_pv(qh.astype(BF16), kt_scr, h * 128, v_scr, (h // 2) * 128, s_scr, nch)
        heads.append(acc[:, (h % 2) * V_C:(h % 2 + 1) * V_C] / l)
    o_ref[...] = jnp.concatenate(heads, axis=1)


def _mla_attn(c_q, c_kv, small, batch, seq_len, wts, past, rope_tabs):
    nq = seq_len // TQ
    nch_own = seq_len // TILE
    has_past = past is not None
    has_rope = rope_tabs is not None
    nch = nch_own + (1 if has_past else 0)
    const = lambda b, i: (0, 0)
    in_specs = [pl.BlockSpec((TQ, Q_RANK), lambda b, i: (b * nq + i, 0)),
                pl.BlockSpec((seq_len, KV_RANK), lambda b, i: (b, 0)),
                pl.BlockSpec((seq_len, 128), lambda b, i: (b, 0))]
    args = [c_q, c_kv, small]
    if has_past:
        in_specs += [pl.BlockSpec((None, TILE, 128), lambda b, i: (b, 0, 0))] * 2
        args += list(past)
    in_specs += [pl.BlockSpec((1, Q_RANK), const), pl.BlockSpec((1, KV_RANK), const),
                 pl.BlockSpec((Q_RANK, 384), const), pl.BlockSpec((KV_RANK, 256), const),
                 pl.BlockSpec((KV_RANK, 256), const)]
    args += list(wts)
    if has_rope:
        in_specs += [pl.BlockSpec((TQ, 128), lambda b, i: (i, 0))] * 3
        args += list(rope_tabs)
    return pl.pallas_call(
        functools.partial(_mla_kernel, has_past, has_rope, nch_own),
        grid=(batch, nq),
        in_specs=in_specs,
        out_specs=[pl.BlockSpec((TQ, 256), lambda b, i: (b * nq + i, 0)),
                   pl.BlockSpec((seq_len, KV_RANK), lambda b, i: (b, 0))],
        out_shape=[jax.ShapeDtypeStruct((batch * seq_len, 256), F32),
                   jax.ShapeDtypeStruct((batch * seq_len, KV_RANK), F32)],
        scratch_shapes=[pltpu.VMEM((nch, 512, TILE), BF16), pltpu.VMEM((nch, TILE, 256), BF16),
                        pltpu.VMEM((nch, TQ, TILE), F32)],
        compiler_params=_cparams(2),
        name="mla_attn",
    )(*args)


def _chunk_masks(reverse):
    i = lax.broadcasted_iota(jnp.int32, (TILE, TILE), 0)
    j = lax.broadcasted_iota(jnp.int32, (TILE, TILE), 1)
    same = (i >> 6) == (j >> 6)
    d = (j - i) if reverse else (i - j)
    causal = jnp.logical_and(same, d >= 0)
    strict = jnp.logical_and(same, d > 0)
    return same, causal, strict


def _chunk_sums(x, xt, same, causal):
    m_run = jnp.where(causal, 1.0, 0.0).astype(BF16)
    m_all = jnp.where(same, 1.0, 0.0).astype(BF16)
    run_c = tot_c = run_r = None
    for part in _split3(x):
        a, b = _dot(m_run, part), _dot(m_all, part)
        run_c = a if run_c is None else run_c + a
        tot_c = b if tot_c is None else tot_c + b
    for part in _split3(xt):
        a = _dot_nt(part, m_run)
        run_r = a if run_r is None else run_r + a
    return run_c, run_r, tot_c


def _conv3(main, prev8, next8, w, has_prev, has_next):
    n = main.shape[0]
    row = lax.broadcasted_iota(jnp.int32, main.shape, 0)
    before = jnp.where(has_prev, prev8[7:8, :], 0.0)
    after = jnp.where(has_next, next8[0:1, :], 0.0)
    xm = jnp.where(row == 0, before, pltpu.roll(main, 1, 0))
    xp = jnp.where(row == n - 1, after, pltpu.roll(main, n - 1, 0))
    return xm * w[0:1, :] + main * w[1:2, :] + xp * w[2:3, :]


def _inv_unit_triangular(a):
    n = a.shape[0]
    i = lax.broadcasted_iota(jnp.int32, (n, n), 0)
    j = lax.broadcasted_iota(jnp.int32, (n, n), 1)
    t = jnp.where(i == j, 1.0, 0.0) - jnp.where((i >> 1) == (j >> 1), a, 0.0)
    for lvl in range(1, CHUNK.bit_length() - 1):
        joins = jnp.logical_and((i >> (lvl + 1)) == (j >> (lvl + 1)), (i >> lvl) != (j >> lvl))
        t = t - _dot3(_dot3(t, jnp.where(joins, a, 0.0)), t)
    return t


def _scan_specs(batch, seq_len, width):
    nt = seq_len // TILE
    last8 = batch * seq_len // 8 - 1

    def tile(b, d, i):
        return b * nt + i + d * (nt - 1 - 2 * i)

    main = pl.BlockSpec((TILE, width), lambda b, d, i: (tile(b, d, i), 0))
    prev = pl.BlockSpec((8, width), lambda b, d, i: (jnp.maximum(tile(b, d, i) * (TILE // 8) - 1, 0), 0))
    nxt = pl.BlockSpec((8, width), lambda b, d, i: (jnp.minimum((tile(b, d, i) + 1) * (TILE // 8), last8), 0))
    return main, prev, nxt, tile


def _deltanet_body(reverse, nt, has_state, refs):
    if has_state:
        main_ref, prev_ref, next_ref, sm_ref, w_ref, prm_ref, s0_ref, o_ref, sfin_ref, s_scr = refs
    else:
        main_ref, prev_ref, next_ref, sm_ref, w_ref, prm_ref, o_ref, sfin_ref, s_scr = refs
    step = pl.program_id(2)
    pos = (nt - 1 - step) if reverse else step
    dirn = 1 if reverse else 0

    @pl.when(step == 0)
    def _init():
        if has_state:
            s_scr[...] = s0_ref[...]
        else:
            s_scr[...] = jnp.zeros(s_scr.shape, F32)

    qkv = _silu(_conv3(main_ref[...], prev_ref[...], next_ref[...], w_ref[...], pos > 0, pos < nt - 1))
    q_all = qkv[:, 0:256]
    k_all = qkv[:, 256:512]
    v_all = qkv[:, 512:768]
    sm = sm_ref[...]
    lane = lax.broadcasted_iota(jnp.int32, sm.shape, 1)
    beta = _sigmoid(sm)
    g = -jnp.exp(prm_ref[0:1, :]) * _softplus(sm + prm_ref[1:2, :])
    g = jnp.where(jnp.logical_and(lane >= COL_DECAY, lane < COL_DECAY + 2 * H_B), g, 0.0)
    same, causal, strict = _chunk_masks(reverse)
    gc_c, gc_r, gt_c = _chunk_sums(g, g.T, same, causal)

    qs, ks, kds, per_head = [], [], [], []
    for h in range(H_B):
        sl = slice(h * DK_B, (h + 1) * DK_B)
        qh, kh = q_all[:, sl], k_all[:, sl]
        qh = qh * lax.rsqrt(jnp.sum(qh * qh, axis=-1, keepdims=True) + EPS) * (DK_B ** -0.5)
        kh = kh * lax.rsqrt(jnp.sum(kh * kh, axis=-1, keepdims=True) + EPS)
        col = COL_DECAY + dirn * H_B + h
        gc = gc_c[:, col:col + 1]
        kds.append(kh * jnp.exp(gt_c[:, col:col + 1] - gc))
        qs.append(qh)
        ks.append(kh)
    kdt = jnp.concatenate(kds, axis=1).T.astype(BF16)

    outs = []
    for h in range(H_B):
        qh, kh = qs[h], ks[h]
        vh = v_all[:, h * DV_B:(h + 1) * DV_B]
        col = COL_DECAY + dirn * H_B + h
        bcol = COL_BETA + dirn * H_B + h
        gc = gc_c[:, col:col + 1]
        gcr = gc_r[col:col + 1, :]
        bh = beta[:, bcol:bcol + 1]
        dec = jnp.exp(jnp.where(causal, gc - gcr, -jnp.inf))
        kb = kh * bh
        k16 = kh.astype(BF16)
        a = jnp.where(strict, _dot_nt(kb.astype(BF16), k16) * dec, 0.0)
        t16 = _inv_unit_triangular(a).astype(BF16)
        egc = jnp.exp(gc)
        u = _dot(t16, (vh * bh).astype(BF16))
        w16 = _dot(t16, (kb * egc).astype(BF16)).astype(BF16)
        qk16 = jnp.where(causal, _dot_nt(qh.astype(BF16), k16) * dec, 0.0).astype(BF16)
        qd16 = (qh * egc).astype(BF16)
        egl = jnp.exp(gt_c[:, col:col + 1])
        s = s_scr[h]
        o_rows = [None] * 4
        for c in (range(3, -1, -1) if reverse else range(4)):
            r = slice(c * CHUNK, (c + 1) * CHUNK)
            s16 = s.astype(BF16)
            v_new = u[r] - _dot(w16[r], s16)
            vn16 = v_new.astype(BF16)
            o_rows[c] = _dot(qd16[r], s16) + _dot(qk16[r, r], vn16)
            s = s * egl[c * CHUNK:c * CHUNK + 1, :] + _dot(kdt[h * DK_B:(h + 1) * DK_B, r], vn16)
        s_scr[h] = s
        outs.append(jnp.concatenate(o_rows, axis=0))
    o_ref[...] = jnp.concatenate(outs, axis=1)

    @pl.when(step == nt - 1)
    def _fin():
        sfin_ref[...] = s_scr[...]


def _deltanet_kernel(nt, has_state, *refs):
    d = pl.program_id(1)

    @pl.when(d == 0)
    def _fwd():
        _deltanet_body(False, nt, has_state, refs)

    @pl.when(d == 1)
    def _bwd():
        _deltanet_body(True, nt, has_state, refs)


def _deltanet(b_qkv, small, batch, seq_len, conv_w, prm, s0):
    nt = seq_len // TILE
    has_state = s0 is not None
    main, prev, nxt, tile = _scan_specs(batch, seq_len, 768)
    const = lambda b, d, i: (0, 0)
    in_specs = [main, prev, nxt, pl.BlockSpec((TILE, 128), lambda b, d, i: (tile(b, d, i), 0)),
                pl.BlockSpec((3, 768), const), pl.BlockSpec((8, 128), const)]
    args = [b_qkv, b_qkv, b_qkv, small, conv_w, prm]
    if has_state:
        in_specs.append(pl.BlockSpec((None, None, H_B, DK_B, DV_B), lambda b, d, i: (b, d, 0, 0, 0)))
        args.append(s0)
    return pl.pallas_call(
        functools.partial(_deltanet_kernel, nt, has_state),
        grid=(batch, 2, nt),
        in_specs=in_specs,
        out_specs=[pl.BlockSpec((None, TILE, 256), lambda b, d, i: (d, tile(b, d, i), 0)),
                   pl.BlockSpec((None, None, H_B, DK_B, DV_B), lambda b, d, i: (b, d, 0, 0, 0))],
        out_shape=[jax.ShapeDtypeStruct((2, batch * seq_len, 256), F32),
                   jax.ShapeDtypeStruct((batch, 2, H_B, DK_B, DV_B), F32)],
        scratch_shapes=[pltpu.VMEM((H_B, DK_B, DV_B), F32)],
        compiler_params=_cparams(3),
        name="deltanet",
    )(*args)


def _ssd_body(reverse, nt, has_state, refs):
    if has_state:
        main_ref, prev_ref, next_ref, sm_ref, w_ref, cb_ref, prm_ref, s0_ref, y_ref, sfin_ref, s_scr = refs
    else:
        main_ref, prev_ref, next_ref, sm_ref, w_ref, cb_ref, prm_ref, y_ref, sfin_ref, s_scr = refs
    step = pl.program_id(2)
    pos = (nt - 1 - step) if reverse else step
    dirn = 1 if reverse else 0

    @pl.when(step == 0)
    def _init():
        if has_state:
            s_scr[...] = s0_ref[...]
        else:
            s_scr[...] = jnp.zeros(s_scr.shape, F32)

    xbc = _silu(_conv3(main_ref[...], prev_ref[...], next_ref[...], w_ref[...], pos > 0, pos < nt - 1) + cb_ref[...])
    x_all = xbc[:, 0:256]
    sm = sm_ref[...]
    lane = lax.broadcasted_iota(jnp.int32, sm.shape, 1)
    dt = _softplus(sm + prm_ref[1:2, 0:128])
    da = jnp.where(jnp.logical_and(lane >= COL_DT, lane < COL_DT + 2 * H_D),
                   dt * (-jnp.exp(prm_ref[0:1, 0:128])), 0.0)
    same, causal, _ = _chunk_masks(reverse)
    ac_c, ac_r, at_c = _chunk_sums(da, da.T, same, causal)

    xdts = []
    for h in range(H_D):
        col = COL_DT + dirn * H_D + h
        xdts.append(x_all[:, h * P_D:(h + 1) * P_D] * dt[:, col:col + 1])
    xdt_t = jnp.concatenate(xdts, axis=1).T.astype(BF16)
    row_chunk = lax.broadcasted_iota(jnp.int32, (TILE, N_D), 0) >> 6

    outs = []
    cb = None
    for h in range(H_D):
        grp = h // (H_D // G_D)
        bm = xbc[:, 256 + grp * N_D:256 + (grp + 1) * N_D]
        cm = xbc[:, 512 + grp * N_D:512 + (grp + 1) * N_D]
        if h % (H_D // G_D) == 0:
            cb = _dot_nt(cm.astype(BF16), bm.astype(BF16))
        col = COL_DT + dirn * H_D + h
        ac = ac_c[:, col:col + 1]
        seg = jnp.exp(jnp.where(causal, ac - ac_r[col:col + 1, :], -jnp.inf))
        xdt16 = xdts[h].astype(BF16)
        y = _dot((cb * seg).astype(BF16), xdt16)
        b_dec = bm * jnp.exp(at_c[:, col:col + 1] - ac)
        cd16 = (cm * jnp.exp(ac)).astype(BF16)
        elast = jnp.exp(at_c[:, col:col + 1])
        s = s_scr[h]
        y_rows = [None] * 4
        for c in (range(3, -1, -1) if reverse else range(4)):
            r = slice(c * CHUNK, (c + 1) * CHUNK)
            y_rows[c] = _dot_nt(cd16[r], s.astype(BF16))
            st = _dot(xdt_t[h * P_D:(h + 1) * P_D, :], jnp.where(row_chunk == c, b_dec, 0.0).astype(BF16))
            s = s * elast[c * CHUNK:c * CHUNK + 1, :] + st
        s_scr[h] = s
        y = y + jnp.concatenate(y_rows, axis=0)
        if not reverse:
            y = y + prm_ref[2:3, h * P_D:(h + 1) * P_D] * x_all[:, h * P_D:(h + 1) * P_D]
        outs.append(y)
    y_ref[...] = jnp.concatenate(outs, axis=1)

    @pl.when(step == nt - 1)
    def _fin():
        sfin_ref[...] = s_scr[...]


def _ssd_kernel(nt, has_state, *refs):
    d = pl.program_id(1)

    @pl.when(d == 0)
    def _fwd():
        _ssd_body(False, nt, has_state, refs)

    @pl.when(d == 1)
    def _bwd():
        _ssd_body(True, nt, has_state, refs)


def _ssd(d_xbc, small, batch, seq_len, conv_w, conv_b, prm, s0):
    nt = seq_len // TILE
    has_state = s0 is not None
    main, prev, nxt, tile = _scan_specs(batch, seq_len, 768)
    const = lambda b, d, i: (0, 0)
    in_specs = [main, prev, nxt, pl.BlockSpec((TILE, 128), lambda b, d, i: (tile(b, d, i), 0)),
                pl.BlockSpec((3, 768), const), pl.BlockSpec((1, 768), const), pl.BlockSpec((8, 256), const)]
    args = [d_xbc, d_xbc, d_xbc, small, conv_w, conv_b, prm]
    if has_state:
        in_specs.append(pl.BlockSpec((None, None, H_D, P_D, N_D), lambda b, d, i: (b, d, 0, 0, 0)))
        args.append(s0)
    return pl.pallas_call(
        functools.partial(_ssd_kernel, nt, has_state),
        grid=(batch, 2, nt),
        in_specs=in_specs,
        out_specs=[pl.BlockSpec((None, TILE, 256), lambda b, d, i: (d, tile(b, d, i), 0)),
                   pl.BlockSpec((None, None, H_D, P_D, N_D), lambda b, d, i: (b, d, 0, 0, 0))],
        out_shape=[jax.ShapeDtypeStruct((2, batch * seq_len, 256), F32),
                   jax.ShapeDtypeStruct((batch, 2, H_D, P_D, N_D), F32)],
        scratch_shapes=[pltpu.VMEM((H_D, P_D, N_D), F32)],
        compiler_params=_cparams(3),
        name="ssd",
    )(*args)


def _outproj_ffn_kernel(x_ref, mod_ref, oa_ref, obf_ref, obb_ref, bg_ref, oc_ref, ydf_ref, ydb_ref, dz_ref,
                        dng_ref, ssg_ref, wo_ref, l1g_ref, l1b_ref, w1_ref, w2_ref, l2g_ref, l2b_ref, o_ref):
    x = x_ref[...]
    o_b = _rms_groups(obf_ref[...] + obb_ref[...], DV_B) * dng_ref[...] * _silu(bg_ref[...])
    y_d = (ydf_ref[...] + ydb_ref[...]) * _silu(dz_ref[...])
    o_d = _rms_groups(y_d, GROUP_W // G_D) * ssg_ref[...]
    mixed = None
    for idx, part in enumerate((oa_ref[...], o_b, oc_ref[...], o_d)):
        y = _dot(part.astype(BF16), wo_ref[idx * GROUP_W:(idx + 1) * GROUP_W, :])
        mixed = y if mixed is None else mixed + y
    x1 = _layer_norm(ALPHA * x + mod_ref[2:3, :] * mixed, l1g_ref[...], l1b_ref[...])
    h16 = (x1 * (1.0 + mod_ref[4:5, :]) + mod_ref[3:4, :]).astype(BF16)
    ff = None
    fcol = 1024
    for j in range(D_FF // fcol):
        a = jnp.square(jnp.maximum(_dot(h16, w1_ref[:, j * fcol:(j + 1) * fcol]), 0.0)).astype(BF16)
        y = _dot(a, w2_ref[j * fcol:(j + 1) * fcol, :])
        ff = y if ff is None else ff + y
    o_ref[...] = _layer_norm(ALPHA * x1 + mod_ref[5:6, :] * ff, l2g_ref[...], l2b_ref[...])


def _outproj_ffn(x, mod, o_a, o_b2, b_gate, o_c, y_d2, d_z, wts, seq_len, per_batch_mod):
    t = x.shape[0]
    tb = 256
    nb_seq = seq_len // tb
    mod_idx = (lambda i: (1 + i // nb_seq, 0, 0)) if per_batch_mod else (lambda i: (0, 0, 0))
    row = lambda wd: pl.BlockSpec((tb, wd), lambda i: (i, 0))
    dir_spec = lambda d: pl.BlockSpec((None, tb, 256), lambda i: (d, i, 0))
    const = lambda i: (0, 0)
    single = pl.Buffered(1)
    in_specs = [row(D_MODEL), pl.BlockSpec((None, 6, D_MODEL), mod_idx),
                row(256), dir_spec(0), dir_spec(1), row(256), row(256), dir_spec(0), dir_spec(1), row(256),
                pl.BlockSpec((1, 256), const), pl.BlockSpec((1, 256), const),
                pl.BlockSpec((D_MODEL, D_MODEL), const, pipeline_mode=single),
                pl.BlockSpec((1, D_MODEL), const), pl.BlockSpec((1, D_MODEL), const),
                pl.BlockSpec((D_MODEL, D_FF), const, pipeline_mode=single),
                pl.BlockSpec((D_FF, D_MODEL), const, pipeline_mode=single),
                pl.BlockSpec((1, D_MODEL), const), pl.BlockSpec((1, D_MODEL), const)]
    return pl.pallas_call(
        _outproj_ffn_kernel,
        grid=(t // tb,),
        in_specs=in_specs,
        out_specs=row(D_MODEL),
        out_shape=jax.ShapeDtypeStruct((t, D_MODEL), F32),
        compiler_params=_cparams(1),
        name="outproj_ffn",
    )(x, mod, o_a, o_b2, o_b2, b_gate, o_c, y_d2, y_d2, d_z, *wts)


def _rope_tables(length, dim, width):
    rows = length // GRID_W
    row_pos = jnp.repeat(jnp.arange(rows, dtype=F32), GRID_W)
    col_pos = jnp.tile(jnp.arange(GRID_W, dtype=F32), rows)
    half = dim // 2
    inv_freq = ROPE_BASE ** (-jnp.arange(0, half, 2, dtype=F32) / half)
    ang_r = row_pos[:, None] * inv_freq
    ang_c = col_pos[:, None] * inv_freq
    ang = jnp.concatenate([ang_r, ang_r, ang_c, ang_c], axis=-1)
    cos, sin = jnp.cos(ang), jnp.sin(ang)
    first = (jnp.arange(dim) % (dim // 2)) < (dim // 4)
    sin_lo = jnp.where(first, -sin, 0.0)
    sin_hi = jnp.where(first, 0.0, sin)
    return cos, sin_lo, sin_hi


def _tile_lanes(tabs, reps):
    return tuple(jnp.tile(t, (1, reps)) for t in tabs)


def _pad_lanes(tabs, width):
    cos, lo, hi = tabs
    pad = width - cos.shape[1]
    return (jnp.pad(cos, ((0, 0), (0, pad)), constant_values=1.0), jnp.pad(lo, ((0, 0), (0, pad))),
            jnp.pad(hi, ((0, 0), (0, pad))))


def _permute_w_in(w_in):
    offs = {}
    o = 0
    for name, size in zip(SPLIT_NAMES, SPLIT_SIZES):
        offs[name] = (o, size)
        o += size
    cols = []
    for names, width in zip(OUT_GROUPS, OUT_WIDTHS):
        used = 0
        for n in names:
            s, size = offs[n]
            cols.append(w_in[:, :, s:s + size])
            used += size
        if used < width:
            cols.append(jnp.zeros(w_in.shape[:2] + (width - used,), w_in.dtype))
    return jnp.concatenate(cols, axis=-1).astype(BF16)


def _param_row(vals, col0, width=128, rows=8):
    out = jnp.zeros((DEPTH, rows, width), F32)
    for r, v in enumerate(vals):
        out = out.at[:, r, col0:col0 + v.shape[1]].set(v.astype(F32))
    return out


def kernel(x_prompt, x_sample, cache_diff_k, cache_diff_v, state_delta, cache_mla_ckv, cache_mla_krope, state_ssm, c, c_ctx, w_mod, b_mod, w_in, diff_lam, diff_norm, dn_conv, dn_a_log, dn_dt_bias, dn_norm, mla_q_norm, mla_kv_norm, mla_w_uq, mla_w_uk, mla_w_uv, ssm_conv_w, ssm_conv_b, ssm_a_log, ssm_dt_bias, ssm_d, ssm_norm, w_out, ln1_g, ln1_b, ln2_g, ln2_b, w_ff1, w_ff2):
    nb_p, len_p = x_prompt.shape[:2]
    nb_s, len_s = x_sample.shape[:2]
    past_len = cache_diff_k.shape[2]
    assert past_len == TILE and nb_s + 1 <= 8

    cvec = jnp.concatenate([c_ctx[None, :], c, jnp.zeros((8 - 1 - nb_s, D_MODEL), F32)], axis=0)
    mods = _mod_call(cvec, w_mod, b_mod).reshape(DEPTH, 8, 6, D_MODEL)

    w_in_p = _permute_w_in(w_in)
    w_uq = mla_w_uq.reshape(DEPTH, Q_RANK, H_C, NOPE_C + ROPE_C)
    w_uq_p = jnp.concatenate([w_uq[..., :NOPE_C].reshape(DEPTH, Q_RANK, H_C * NOPE_C),
                              w_uq[..., NOPE_C:].reshape(DEPTH, Q_RANK, H_C * ROPE_C)], axis=-1).astype(BF16)
    w_uk16, w_uv16 = mla_w_uk.astype(BF16), mla_w_uv.astype(BF16)
    w_out16, w_ff1_16, w_ff2_16 = w_out.astype(BF16), w_ff1.astype(BF16), w_ff2.astype(BF16)
    diff_g = jnp.tile(diff_norm, (1, H_A))[:, None, :]
    dn_g = jnp.tile(dn_norm, (1, H_B))[:, None, :]
    ss_g = ssm_norm[:, None, :]
    dn_prm = _param_row([dn_a_log.reshape(DEPTH, 2 * H_B), dn_dt_bias.reshape(DEPTH, 2 * H_B)], COL_DECAY)
    ssm_prm = _param_row([ssm_a_log.reshape(DEPTH, 2 * H_D), ssm_dt_bias.reshape(DEPTH, 2 * H_D)], COL_DT, width=256)
    ssm_prm = ssm_prm.at[:, 2, :].set(jnp.repeat(ssm_d, P_D, axis=1))

    tabs_a = _tile_lanes(_rope_tables(len_s, DQK_A, 256), 256 // DQK_A)
    tabs_c = _rope_tables(len_s, ROPE_C, 128)
    tabs_small = _pad_lanes(tabs_c, 128)
    tabs_q = _tile_lanes(tabs_c, 128 // ROPE_C)

    past_kr = jnp.pad(cache_mla_krope, ((0, 0), (0, 0), (0, 0), (0, 128 - ROPE_C)))

    def layer(x, l, batch, seq_len, is_latent):
        mod = mods[l]
        (a_qkv, b_qkv, b_gate, c_q, d_z, d_xbc, c_kv, small) = _inproj(
            x, mod, w_in_p[l], seq_len, is_latent, tabs_a + tabs_small if is_latent else None)
        if is_latent:
            past_a = (cache_diff_k[:, l].reshape(batch, past_len, 256), cache_diff_v[:, l].reshape(batch, past_len, 256))
            past_c = (cache_mla_ckv[:, l], past_kr[:, l])
            s0_b, s0_d = state_delta[:, l], state_ssm[:, l]
        else:
            past_a = past_c = s0_b = s0_d = None
        o_a = _diff_attn(a_qkv, batch, seq_len, l, diff_lam[l], diff_g[l], past_a)
        o_c, ckv_n = _mla_attn(c_q, c_kv, small, batch, seq_len,
                               (mla_q_norm[l][None, :], mla_kv_norm[l][None, :], w_uq_p[l], w_uk16[l], w_uv16[l]),
                               past_c, tabs_q if is_latent else None)
        o_b2, st_b = _deltanet(b_qkv, small, batch, seq_len, dn_conv[l], dn_prm[l], s0_b)
        y_d2, st_d = _ssd(d_xbc, small, batch, seq_len, ssm_conv_w[l], ssm_conv_b[l][None, :], ssm_prm[l], s0_d)
        x = _outproj_ffn(x, mod, o_a, o_b2, b_gate, o_c, y_d2, d_z,
                         (dn_g[l], ss_g[l], w_out16[l], ln1_g[l][None, :], ln1_b[l][None, :], w_ff1_16[l], w_ff2_16[l],
                          ln2_g[l][None, :], ln2_b[l][None, :]), seq_len, is_latent)
        return x, (a_qkv, st_b, ckv_n, small, st_d)

    xp = x_prompt.reshape(nb_p * len_p, D_MODEL)
    xs = x_sample.reshape(nb_s * len_s, D_MODEL)
    ctx = ([], [], [], [], [], [])
    for l in range(DEPTH):
        xp, (a_qkv, st_b, ckv_n, small, st_d) = layer(xp, l, nb_p, len_p, False)
        xs, _ = layer(xs, l, nb_s, len_s, True)
        ctx[0].append(a_qkv[:, 256:512].reshape(nb_p, len_p, H_A, 2, DQK_A))
        ctx[1].append(a_qkv[:, 512:768].reshape(nb_p, len_p, H_A, DV_A))
        ctx[2].append(st_b)
        ctx[3].append(ckv_n.reshape(nb_p, len_p, KV_RANK))
        ctx[4].append(small[:, COL_KR:COL_KR + ROPE_C].reshape(nb_p, len_p, ROPE_C))
        ctx[5].append(st_d)
    return (xp.reshape(nb_p, len_p, D_MODEL), xs.reshape(nb_s, len_s, D_MODEL)) + tuple(
        jnp.stack(t, axis=1) for t in ctx)
```

```python
import functools
import math

import jax
import jax.numpy as jnp
from jax import lax
from jax.experimental import pallas as pl
from jax.experimental.pallas import tpu as pltpu

F32 = jnp.float32
BF16 = jnp.bfloat16

D_MODEL = 1024
DEPTH = 4
GRID_W = 64
GROUP_W = 256
H_A, DV_A, DQK_A = 4, 64, 32
H_B, DK_B, DV_B = 4, 64, 64
H_C, NOPE_C, ROPE_C, V_C = 4, 64, 32, 64
Q_RANK, KV_RANK = 256, 128
H_D, P_D, N_D, G_D = 4, 64, 128, 2
CHUNK = 64
D_FF = 4 * D_MODEL
ROPE_BASE = 10000.0
EPS = 1e-6
ALPHA = (2 * DEPTH) ** 0.25
LOG2E = 1.4426950408889634

SPLIT_NAMES = ("a_q", "a_k", "a_v", "b_qkv", "b_beta", "b_decay", "b_gate",
               "c_q", "c_kv", "c_kr", "d_z", "d_xbc", "d_dt")
SPLIT_SIZES = (256, 256, 256, 768, 8, 8, 256, 256, 128, 32, 256, 768, 8)
OUT_GROUPS = (("a_q", "a_k", "a_v"), ("b_qkv",), ("b_gate",), ("c_q",), ("d_z",), ("d_xbc",),
              ("c_kv",), ("c_kr", "b_beta", "b_decay", "d_dt"))
OUT_WIDTHS = (768, 768, 256, 256, 256, 768, 128, 128)
IN_COLS_PAD = sum(OUT_WIDTHS)
COL_KR, COL_BETA, COL_DECAY, COL_DT = 0, 32, 40, 48

TILE = 256
TQ = 256
TB = 512
VMEM_LIMIT = 56 * 1024 * 1024


def _cparams(n_axes):
    return pltpu.CompilerParams(dimension_semantics=("arbitrary",) * n_axes,
                                vmem_limit_bytes=VMEM_LIMIT)


def _dot(a, b):
    return jnp.dot(a, b, preferred_element_type=F32)


def _dot_nt(a, b):
    return lax.dot_general(a, b, (((1,), (1,)), ((), ())), preferred_element_type=F32)


def _split3(x):
    x1 = x.astype(BF16)
    r = x - x1.astype(F32)
    x2 = r.astype(BF16)
    x3 = (r - x2.astype(F32)).astype(BF16)
    return x1, x2, x3


def _sigmoid(x):
    return 1.0 / (1.0 + jnp.exp(-x))


def _silu(x):
    return x * _sigmoid(x)


def _softplus(x):
    return jnp.maximum(x, 0.0) + jnp.log1p(jnp.exp(-jnp.abs(x)))


def _rms_groups(x, width):
    outs = []
    for s in range(0, x.shape[1], width):
        xg = x[:, s:s + width]
        ms = jnp.mean(xg * xg, axis=-1, keepdims=True)
        outs.append(xg * lax.rsqrt(ms + EPS))
    return outs[0] if len(outs) == 1 else jnp.concatenate(outs, axis=1)


def _layer_norm(x, g, b):
    mu = jnp.mean(x, axis=-1, keepdims=True)
    xc = x - mu
    var = jnp.mean(xc * xc, axis=-1, keepdims=True)
    return xc * lax.rsqrt(var + EPS) * g + b


def _rope(x, cos, sin_lo, sin_hi):
    w = x.shape[1]
    return x * cos + pltpu.roll(x, w - 8, 1) * sin_lo + pltpu.roll(x, 8, 1) * sin_hi


def _mod_kernel(c_ref, w_ref, b_ref, o_ref):
    c = c_ref[...]
    o_ref[...] = _dot(_silu(c).astype(BF16), w_ref[...].astype(BF16)) + b_ref[...]


def _mod_call(cvec, w_mod, b_mod):
    nblk = 4
    wcol = 6 * D_MODEL // nblk
    return pl.pallas_call(
        _mod_kernel,
        grid=(DEPTH, nblk),
        in_specs=[pl.BlockSpec((8, D_MODEL), lambda l, j: (0, 0)),
                  pl.BlockSpec((None, D_MODEL, wcol), lambda l, j: (l, 0, j)),
                  pl.BlockSpec((None, 1, wcol), lambda l, j: (l, 0, j))],
        out_specs=pl.BlockSpec((None, 8, wcol), lambda l, j: (l, 0, j)),
        out_shape=jax.ShapeDtypeStruct((DEPTH, 8, 6 * D_MODEL), F32),
        compiler_params=_cparams(2),
        name="mod",
    )(cvec, w_mod, b_mod.reshape(DEPTH, 1, 6 * D_MODEL))


def _inproj_kernel(has_rope, *refs):
    if has_rope:
        (x_ref, mod_ref, w_ref, cos_ref, slo_ref, shi_ref, cos_s_ref, slo_s_ref, shi_s_ref,
         oa, obq, obg, ocq, odz, odx, ockv, osm) = refs
    else:
        x_ref, mod_ref, w_ref, oa, obq, obg, ocq, odz, odx, ockv, osm = refs
    x = x_ref[...]
    h = (x * (1.0 + mod_ref[1:2, :]) + mod_ref[0:1, :]).astype(BF16)
    outs = (oa, obq, obg, ocq, odz, odx, ockv, osm)
    off = 0
    for idx, (o_ref, wd) in enumerate(zip(outs, OUT_WIDTHS)):
        y = _dot(h, w_ref[:, off:off + wd])
        off += wd
        if has_rope and idx == 0:
            cos, slo, shi = cos_ref[...], slo_ref[...], shi_ref[...]
            o_ref[:, 0:256] = _rope(y[:, 0:256], cos, slo, shi)
            o_ref[:, 256:512] = _rope(y[:, 256:512], cos, slo, shi)
            o_ref[:, 512:768] = y[:, 512:768]
        elif has_rope and idx == 7:
            o_ref[...] = _rope(y, cos_s_ref[...], slo_s_ref[...], shi_s_ref[...])
        else:
            o_ref[...] = y


def _inproj(x, mod, w_in_p, seq_len, per_batch_mod, rope_tabs):
    t = x.shape[0]
    nb_seq = seq_len // TB
    has_rope = rope_tabs is not None
    mod_idx = (lambda i: (1 + i // nb_seq, 0, 0)) if per_batch_mod else (lambda i: (0, 0, 0))
    in_specs = [pl.BlockSpec((TB, D_MODEL), lambda i: (i, 0)),
                pl.BlockSpec((None, 6, D_MODEL), mod_idx),
                pl.BlockSpec((D_MODEL, IN_COLS_PAD), lambda i: (0, 0))]
    args = [x, mod, w_in_p]
    if has_rope:
        for tab in rope_tabs:
            in_specs.append(pl.BlockSpec((TB, tab.shape[1]), lambda i: (i % nb_seq, 0)))
            args.append(tab)
    return pl.pallas_call(
        functools.partial(_inproj_kernel, has_rope),
        grid=(t // TB,),
        in_specs=in_specs,
        out_specs=[pl.BlockSpec((TB, wd), lambda i: (i, 0)) for wd in OUT_WIDTHS],
        out_shape=[jax.ShapeDtypeStruct((t, wd), F32) for wd in OUT_WIDTHS],
        compiler_params=_cparams(1),
        name="inproj",
    )(*args)


def _softmax_probs(q16, kt_ref, k_row0):
    s = _dot(q16, kt_ref[k_row0:k_row0 + q16.shape[1], :])
    return jnp.exp2(s - jnp.max(s, axis=-1, keepdims=True)).astype(BF16)


def _with_ones(v):
    return jnp.concatenate([v, jnp.ones((v.shape[0], 128 - v.shape[1]), v.dtype)], axis=1).astype(BF16)


def _diff_attn_kernel(has_past, nch_own, tq, *refs):
    if has_past:
        q_ref, k_ref, v_ref, pk_ref, pv_ref, lam_ref, g_ref, o_ref, kt_scr, v_scr = refs
    else:
        q_ref, k_ref, v_ref, lam_ref, g_ref, o_ref, kt_scr, v_scr = refs

    @pl.when(pl.program_id(1) == 0)
    def _prep():
        def put(c, k, v):
            cols = slice(c * TILE, (c + 1) * TILE)
            kt_scr[:, cols] = k.T.astype(BF16)
            for h in range(H_A):
                v_scr[h, cols, :] = _with_ones(v[:, h * DV_A:(h + 1) * DV_A])

        for c in range(nch_own):
            put(c, k_ref[c * TILE:(c + 1) * TILE, :], v_ref[c * TILE:(c + 1) * TILE, :])
        if has_past:
            put(nch_own, pk_ref[...], pv_ref[...])

    lp = lam_ref[...]
    lam_init = lam_ref[4:5, 0:1]
    lam = (jnp.exp(jnp.sum(lp[0:1] * lp[1:2], axis=-1, keepdims=True))
           - jnp.exp(jnp.sum(lp[2:3] * lp[3:4], axis=-1, keepdims=True)) + lam_init)
    q = (q_ref[...] * (DQK_A ** -0.5 * LOG2E)).astype(BF16)
    heads = []
    for h in range(H_A):
        c0 = h * 2 * DQK_A
        p = jnp.concatenate([_softmax_probs(q[:, c0:c0 + DQK_A], kt_scr, c0),
                             _softmax_probs(q[:, c0 + DQK_A:c0 + 2 * DQK_A], kt_scr, c0 + DQK_A)], axis=0)
        acc = _dot(p, v_scr[h])
        o1 = acc[:tq, :DV_A] / acc[:tq, DV_A:DV_A + 1]
        o2 = acc[tq:, :DV_A] / acc[tq:, DV_A:DV_A + 1]
        heads.append(o1 - lam * o2)
    o = jnp.concatenate(heads, axis=1)
    o_ref[...] = _rms_groups(o, DV_A) * g_ref[...] * (1.0 - lam_init)


def _diff_attn(a_qkv, batch, seq_len, lam_p, g_row, past):
    tq = min(TQ, seq_len)
    nq = seq_len // tq
    nch_own = seq_len // TILE
    has_past = past is not None
    n_keys = seq_len + (TILE if has_past else 0)
    in_specs = [pl.BlockSpec((tq, 256), lambda b, i: (b * nq + i, 0)),
                pl.BlockSpec((seq_len, 256), lambda b, i: (b, 1)),
                pl.BlockSpec((seq_len, 256), lambda b, i: (b, 2))]
    args = [a_qkv, a_qkv, a_qkv]
    if has_past:
        in_specs += [pl.BlockSpec((None, TILE, 256), lambda b, i: (b, 0, 0))] * 2
        args += list(past)
    in_specs += [pl.BlockSpec((8, DQK_A), lambda b, i: (0, 0)), pl.BlockSpec((1, 256), lambda b, i: (0, 0))]
    args += [lam_p, g_row]
    return pl.pallas_call(
        functools.partial(_diff_attn_kernel, has_past, nch_own, tq),
        grid=(batch, nq),
        in_specs=in_specs,
        out_specs=pl.BlockSpec((tq, 256), lambda b, i: (b * nq + i, 0)),
        out_shape=jax.ShapeDtypeStruct((batch * seq_len, 256), F32),
        scratch_shapes=[pltpu.VMEM((256, n_keys), BF16), pltpu.VMEM((H_A, n_keys, 128), BF16)],
        compiler_params=_cparams(2),
        name="diff_attn",
    )(*args)


def _mla_kernel(has_past, has_rope, nch_own, tq, *refs):
    refs = list(refs)
    cq_ref, ckv_ref, sm_ref = refs[:3]
    pos = 3
    if has_past:
        pckv_ref, pkr_ref = refs[pos:pos + 2]
        pos += 2
    qn_ref, kvn_ref, wuq_ref, wuk_ref, wuv_ref = refs[pos:pos + 5]
    pos += 5
    if has_rope:
        cos_ref, slo_ref, shi_ref = refs[pos:pos + 3]
        pos += 3
    o_ref, ckv_out = refs[pos:pos + 2]
    kt_scr, v_scr = refs[pos + 2:]

    def put(c, ckv, kr):
        cols = slice(c * TILE, (c + 1) * TILE)
        c16 = ckv.astype(BF16)
        knt = _dot(c16, wuk_ref[...]).T
        v = _dot(c16, wuv_ref[...])
        lane = lax.broadcasted_iota(jnp.int32, kr.shape, 1)
        krt = jnp.where(lane < ROPE_C, kr, 0.0).T[0:64, :].astype(BF16)
        for h in range(H_C):
            kt_scr[h * 128:h * 128 + NOPE_C, cols] = knt[h * NOPE_C:(h + 1) * NOPE_C, :].astype(BF16)
            kt_scr[h * 128 + NOPE_C:(h + 1) * 128, cols] = krt
            v_scr[h, cols, :] = _with_ones(v[:, h * V_C:(h + 1) * V_C])

    @pl.when(pl.program_id(1) == 0)
    def _prep():
        for c in range(nch_own):
            rows = slice(c * TILE, (c + 1) * TILE)
            ckv = _rms_groups(ckv_ref[rows, :], KV_RANK) * kvn_ref[...]
            ckv_out[rows, :] = ckv
            put(c, ckv, sm_ref[rows, :])
        if has_past:
            put(nch_own, pckv_ref[...], pkr_ref[...])

    cqn = (_rms_groups(cq_ref[...], Q_RANK) * qn_ref[...]).astype(BF16)
    qa = _dot(cqn, wuq_ref[...])
    q_nope = qa[:, 0:256]
    q_rope = qa[:, 256:384]
    if has_rope:
        q_rope = _rope(q_rope, cos_ref[...], slo_ref[...], shi_ref[...])
    scale = (NOPE_C + ROPE_C) ** -0.5 * LOG2E
    zpad = jnp.zeros((tq, 128 - NOPE_C - ROPE_C), F32)
    heads = []
    for h in range(H_C):
        qh = jnp.concatenate([q_nope[:, h * NOPE_C:(h + 1) * NOPE_C],
                              q_rope[:, h * ROPE_C:(h + 1) * ROPE_C], zpad], axis=1) * scale
        acc = _dot(_softmax_probs(qh.astype(BF16), kt_scr, h * 128), v_scr[h])
        heads.append(acc[:, :V_C] / acc[:, V_C:V_C + 1])
    o_ref[...] = jnp.concatenate(heads, axis=1)


def _mla_attn(c_q, c_kv, small, batch, seq_len, wts, past, rope_tabs):
    tq = min(TQ, seq_len)
    nq = seq_len // tq
    nch_own = seq_len // TILE
    has_past = past is not None
    has_rope = rope_tabs is not None
    n_keys = seq_len + (TILE if has_past else 0)
    const = lambda b, i: (0, 0)
    in_specs = [pl.BlockSpec((tq, Q_RANK), lambda b, i: (b * nq + i, 0)),
                pl.BlockSpec((seq_len, KV_RANK), lambda b, i: (b, 0)),
                pl.BlockSpec((seq_len, 128), lambda b, i: (b, 0))]
    args = [c_q, c_kv, small]
    if has_past:
        in_specs += [pl.BlockSpec((None, TILE, 128), lambda b, i: (b, 0, 0))] * 2
        args += list(past)
    in_specs += [pl.BlockSpec((1, Q_RANK), const), pl.BlockSpec((1, KV_RANK), const),
                 pl.BlockSpec((Q_RANK, 384), const), pl.BlockSpec((KV_RANK, 256), const),
                 pl.BlockSpec((KV_RANK, 256), const)]
    args += list(wts)
    if has_rope:
        in_specs += [pl.BlockSpec((tq, 128), lambda b, i: (i, 0))] * 3
        args += list(rope_tabs)
    return pl.pallas_call(
        functools.partial(_mla_kernel, has_past, has_rope, nch_own, tq),
        grid=(batch, nq),
        in_specs=in_specs,
        out_specs=[pl.BlockSpec((tq, 256), lambda b, i: (b * nq + i, 0)),
                   pl.BlockSpec((seq_len, KV_RANK), lambda b, i: (b, 0))],
        out_shape=[jax.ShapeDtypeStruct((batch * seq_len, 256), F32),
                   jax.ShapeDtypeStruct((batch * seq_len, KV_RANK), F32)],
        scratch_shapes=[pltpu.VMEM((512, n_keys), BF16), pltpu.VMEM((H_C, n_keys, 128), BF16)],
        compiler_params=_cparams(2),
        name="mla_attn",
    )(*args)


def _chunk_masks(reverse):
    i = lax.broadcasted_iota(jnp.int32, (TILE, TILE), 0)
    j = lax.broadcasted_iota(jnp.int32, (TILE, TILE), 1)
    same = (i >> 6) == (j >> 6)
    d = (j - i) if reverse else (i - j)
    causal = jnp.logical_and(same, d >= 0)
    strict = jnp.logical_and(same, d > 0)
    return same, causal, strict


def _chunk_sums(x, xt, same, causal):
    m_run = jnp.where(causal, 1.0, 0.0).astype(BF16)
    m_all = jnp.where(same, 1.0, 0.0).astype(BF16)
    run_c = tot_c = run_r = None
    for part in _split3(x):
        a, b = _dot(m_run, part), _dot(m_all, part)
        run_c = a if run_c is None else run_c + a
        tot_c = b if tot_c is None else tot_c + b
    for part in _split3(xt):
        a = _dot_nt(part, m_run)
        run_r = a if run_r is None else run_r + a
    return run_c, run_r, tot_c


def _conv3(main, prev8, next8, w, has_prev, has_next):
    n = main.shape[0]
    row = lax.broadcasted_iota(jnp.int32, main.shape, 0)
    before = jnp.where(has_prev, prev8[7:8, :], 0.0)
    after = jnp.where(has_next, next8[0:1, :], 0.0)
    xm = jnp.where(row == 0, before, pltpu.roll(main, 1, 0))
    xp = jnp.where(row == n - 1, after, pltpu.roll(main, n - 1, 0))
    return xm * w[0:1, :] + main * w[1:2, :] + xp * w[2:3, :]


def _level_masks():
    i = lax.broadcasted_iota(jnp.int32, (TILE, TILE), 0)
    j = lax.broadcasted_iota(jnp.int32, (TILE, TILE), 1)
    return [jnp.logical_and((i >> (lvl + 1)) == (j >> (lvl + 1)), (i >> lvl) != (j >> lvl))
            for lvl in range(CHUNK.bit_length() - 1)]


def _inv_unit_triangular_offdiag(a, masks):
    r = -jnp.where(masks[0], a, 0.0)
    for m in masks[1:]:
        e = jnp.where(m, a, 0.0)
        x = e + _dot(r.astype(BF16), e.astype(BF16))
        r = r - x - _dot(x.astype(BF16), r.astype(BF16))
    return r


def _scan_specs(batch, seq_len, width):
    nt = seq_len // TILE
    last8 = batch * seq_len // 8 - 1

    def tile(b, d, i):
        return b * nt + i + d * (nt - 1 - 2 * i)

    main = pl.BlockSpec((TILE, width), lambda b, d, i: (tile(b, d, i), 0))
    prev = pl.BlockSpec((8, width), lambda b, d, i: (jnp.maximum(tile(b, d, i) * (TILE // 8) - 1, 0), 0))
    nxt = pl.BlockSpec((8, width), lambda b, d, i: (jnp.minimum((tile(b, d, i) + 1) * (TILE // 8), last8), 0))
    return main, prev, nxt, tile


def _deltanet_body(reverse, nt, has_state, refs):
    if has_state:
        main_ref, prev_ref, next_ref, sm_ref, w_ref, prm_ref, s0_ref, o_ref, sfin_ref, s_scr = refs
    else:
        main_ref, prev_ref, next_ref, sm_ref, w_ref, prm_ref, o_ref, sfin_ref, s_scr = refs
    step = pl.program_id(2)
    pos = (nt - 1 - step) if reverse else step
    dirn = 1 if reverse else 0

    @pl.when(step == 0)
    def _init():
        if has_state:
            s_scr[...] = s0_ref[...]
        else:
            s_scr[...] = jnp.zeros(s_scr.shape, F32)

    qkv = _silu(_conv3(main_ref[...], prev_ref[...], next_ref[...], w_ref[...], pos > 0, pos < nt - 1))
    q_all = qkv[:, 0:256]
    k_all = qkv[:, 256:512]
    v_all = qkv[:, 512:768]
    sm = sm_ref[...]
    lane = lax.broadcasted_iota(jnp.int32, sm.shape, 1)
    beta = _sigmoid(sm)
    g = -jnp.exp(prm_ref[0:1, :]) * _softplus(sm + prm_ref[1:2, :])
    g = jnp.where(jnp.logical_and(lane >= COL_DECAY, lane < COL_DECAY + 2 * H_B), g, 0.0)
    same, causal, strict = _chunk_masks(reverse)
    masks = _level_masks()
    gc_c, gc_r, gt_c = _chunk_sums(g, g.T, same, causal)

    qs, ks, kds = [], [], []
    for h in range(H_B):
        sl = slice(h * DK_B, (h + 1) * DK_B)
        qh, kh = q_all[:, sl], k_all[:, sl]
        qh = qh * lax.rsqrt(jnp.sum(qh * qh, axis=-1, keepdims=True) + EPS) * (DK_B ** -0.5)
        kh = kh * lax.rsqrt(jnp.sum(kh * kh, axis=-1, keepdims=True) + EPS)
        col = COL_DECAY + dirn * H_B + h
        gc = gc_c[:, col:col + 1]
        kds.append(kh * jnp.exp(gt_c[:, col:col + 1] - gc))
        qs.append(qh)
        ks.append(kh)
    kdt = jnp.concatenate(kds, axis=1).T.astype(BF16)

    outs = []
    for h in range(H_B):
        qh, kh = qs[h], ks[h]
        vh = v_all[:, h * DV_B:(h + 1) * DV_B]
        col = COL_DECAY + dirn * H_B + h
        bcol = COL_BETA + dirn * H_B + h
        gc = gc_c[:, col:col + 1]
        gcr = gc_r[col:col + 1, :]
        bh = beta[:, bcol:bcol + 1]
        dec = jnp.exp(jnp.where(causal, gc - gcr, -jnp.inf))
        kb = kh * bh
        k16 = kh.astype(BF16)
        a = jnp.where(strict, _dot_nt(kb.astype(BF16), k16) * dec, 0.0)
        r16 = _inv_unit_triangular_offdiag(a, masks).astype(BF16)
        egc = jnp.exp(gc)
        vb, kbe = vh * bh, kb * egc
        u = vb + _dot(r16, vb.astype(BF16))
        w16 = (kbe + _dot(r16, kbe.astype(BF16))).astype(BF16)
        qk16 = jnp.where(causal, _dot_nt(qh.astype(BF16), k16) * dec, 0.0).astype(BF16)
        qd16 = (qh * egc).astype(BF16)
        egl = jnp.exp(gt_c[:, col:col + 1])
        s = s_scr[h]
        o_rows = [None] * 4
        for c in (range(3, -1, -1) if reverse else range(4)):
            r = slice(c * CHUNK, (c + 1) * CHUNK)
            s16 = s.astype(BF16)
            v_new = u[r] - _dot(w16[r], s16)
            vn16 = v_new.astype(BF16)
            o_rows[c] = _dot(qd16[r], s16) + _dot(qk16[r, r], vn16)
            s = s * egl[c * CHUNK:c * CHUNK + 1, :] + _dot(kdt[h * DK_B:(h + 1) * DK_B, r], vn16)
        s_scr[h] = s
        outs.append(jnp.concatenate(o_rows, axis=0))
    o_ref[...] = jnp.concatenate(outs, axis=1)

    @pl.when(step == nt - 1)
    def _fin():
        sfin_ref[...] = s_scr[...]


def _deltanet_kernel(nt, has_state, *refs):
    d = pl.program_id(1)

    @pl.when(d == 0)
    def _fwd():
        _deltanet_body(False, nt, has_state, refs)

    @pl.when(d == 1)
    def _bwd():
        _deltanet_body(True, nt, has_state, refs)


def _deltanet(b_qkv, small, batch, seq_len, conv_w, prm, s0):
    nt = seq_len // TILE
    has_state = s0 is not None
    main, prev, nxt, tile = _scan_specs(batch, seq_len, 768)
    const = lambda b, d, i: (0, 0)
    in_specs = [main, prev, nxt, pl.BlockSpec((TILE, 128), lambda b, d, i: (tile(b, d, i), 0)),
                pl.BlockSpec((3, 768), const), pl.BlockSpec((8, 128), const)]
    args = [b_qkv, b_qkv, b_qkv, small, conv_w, prm]
    if has_state:
        in_specs.append(pl.BlockSpec((None, None, H_B, DK_B, DV_B), lambda b, d, i: (b, d, 0, 0, 0)))
        args.append(s0)
    return pl.pallas_call(
        functools.partial(_deltanet_kernel, nt, has_state),
        grid=(batch, 2, nt),
        in_specs=in_specs,
        out_specs=[pl.BlockSpec((None, TILE, 256), lambda b, d, i: (d, tile(b, d, i), 0)),
                   pl.BlockSpec((None, None, H_B, DK_B, DV_B), lambda b, d, i: (b, d, 0, 0, 0))],
        out_shape=[jax.ShapeDtypeStruct((2, batch * seq_len, 256), F32),
                   jax.ShapeDtypeStruct((batch, 2, H_B, DK_B, DV_B), F32)],
        scratch_shapes=[pltpu.VMEM((H_B, DK_B, DV_B), F32)],
        compiler_params=_cparams(3),
        name="deltanet",
    )(*args)


def _ssd_body(reverse, nt, has_state, refs):
    if has_state:
        main_ref, prev_ref, next_ref, sm_ref, w_ref, cb_ref, prm_ref, s0_ref, y_ref, sfin_ref, s_scr = refs
    else:
        main_ref, prev_ref, next_ref, sm_ref, w_ref, cb_ref, prm_ref, y_ref, sfin_ref, s_scr = refs
    step = pl.program_id(2)
    pos = (nt - 1 - step) if reverse else step
    dirn = 1 if reverse else 0

    @pl.when(step == 0)
    def _init():
        if has_state:
            s_scr[...] = s0_ref[...]
        else:
            s_scr[...] = jnp.zeros(s_scr.shape, F32)

    xbc = _silu(_conv3(main_ref[...], prev_ref[...], next_ref[...], w_ref[...], pos > 0, pos < nt - 1) + cb_ref[...])
    x_all = xbc[:, 0:256]
    sm = sm_ref[...]
    lane = lax.broadcasted_iota(jnp.int32, sm.shape, 1)
    dt = _softplus(sm + prm_ref[1:2, 0:128])
    da = jnp.where(jnp.logical_and(lane >= COL_DT, lane < COL_DT + 2 * H_D),
                   dt * (-jnp.exp(prm_ref[0:1, 0:128])), 0.0)
    same, causal, _ = _chunk_masks(reverse)
    ac_c, ac_r, at_c = _chunk_sums(da, da.T, same, causal)

    xdts = []
    for h in range(H_D):
        col = COL_DT + dirn * H_D + h
        xdts.append(x_all[:, h * P_D:(h + 1) * P_D] * dt[:, col:col + 1])
    xdt_t = jnp.concatenate(xdts, axis=1).T.astype(BF16)
    row_chunk = lax.broadcasted_iota(jnp.int32, (TILE, N_D), 0) >> 6

    outs = []
    cb = None
    for h in range(H_D):
        grp = h // (H_D // G_D)
        bm = xbc[:, 256 + grp * N_D:256 + (grp + 1) * N_D]
        cm = xbc[:, 512 + grp * N_D:512 + (grp + 1) * N_D]
        if h % (H_D // G_D) == 0:
            cb = _dot_nt(cm.astype(BF16), bm.astype(BF16))
        col = COL_DT + dirn * H_D + h
        ac = ac_c[:, col:col + 1]
        seg = jnp.exp(jnp.where(causal, ac - ac_r[col:col + 1, :], -jnp.inf))
        xdt16 = xdts[h].astype(BF16)
        y = _dot((cb * seg).astype(BF16), xdt16)
        b_dec = bm * jnp.exp(at_c[:, col:col + 1] - ac)
        cd16 = (cm * jnp.exp(ac)).astype(BF16)
        elast = jnp.exp(at_c[:, col:col + 1])
        s = s_scr[h]
        y_rows = [None] * 4
        for c in (range(3, -1, -1) if reverse else range(4)):
            r = slice(c * CHUNK, (c + 1) * CHUNK)
            y_rows[c] = _dot_nt(cd16[r], s.astype(BF16))
            st = _dot(xdt_t[h * P_D:(h + 1) * P_D, :], jnp.where(row_chunk == c, b_dec, 0.0).astype(BF16))
            s = s * elast[c * CHUNK:c * CHUNK + 1, :] + st
        s_scr[h] = s
        y = y + jnp.concatenate(y_rows, axis=0)
        if not reverse:
            y = y + prm_ref[2:3, h * P_D:(h + 1) * P_D] * x_all[:, h * P_D:(h + 1) * P_D]
        outs.append(y)
    y_ref[...] = jnp.concatenate(outs, axis=1)

    @pl.when(step == nt - 1)
    def _fin():
        sfin_ref[...] = s_scr[...]


def _ssd_kernel(nt, has_state, *refs):
    d = pl.program_id(1)

    @pl.when(d == 0)
    def _fwd():
        _ssd_body(False, nt, has_state, refs)

    @pl.when(d == 1)
    def _bwd():
        _ssd_body(True, nt, has_state, refs)


def _ssd(d_xbc, small, batch, seq_len, conv_w, conv_b, prm, s0):
    nt = seq_len // TILE
    has_state = s0 is not None
    main, prev, nxt, tile = _scan_specs(batch, seq_len, 768)
    const = lambda b, d, i: (0, 0)
    in_specs = [main, prev, nxt, pl.BlockSpec((TILE, 128), lambda b, d, i: (tile(b, d, i), 0)),
                pl.BlockSpec((3, 768), const), pl.BlockSpec((1, 768), const), pl.BlockSpec((8, 256), const)]
    args = [d_xbc, d_xbc, d_xbc, small, conv_w, conv_b, prm]
    if has_state:
        in_specs.append(pl.BlockSpec((None, None, H_D, P_D, N_D), lambda b, d, i: (b, d, 0, 0, 0)))
        args.append(s0)
    return pl.pallas_call(
        functools.partial(_ssd_kernel, nt, has_state),
        grid=(batch, 2, nt),
        in_specs=in_specs,
        out_specs=[pl.BlockSpec((None, TILE, 256), lambda b, d, i: (d, tile(b, d, i), 0)),
                   pl.BlockSpec((None, None, H_D, P_D, N_D), lambda b, d, i: (b, d, 0, 0, 0))],
        out_shape=[jax.ShapeDtypeStruct((2, batch * seq_len, 256), F32),
                   jax.ShapeDtypeStruct((batch, 2, H_D, P_D, N_D), F32)],
        scratch_shapes=[pltpu.VMEM((H_D, P_D, N_D), F32)],
        compiler_params=_cparams(3),
        name="ssd",
    )(*args)


def _outproj_ffn_kernel(x_ref, mod_ref, oa_ref, obf_ref, obb_ref, bg_ref, oc_ref, ydf_ref, ydb_ref, dz_ref,
                        dng_ref, ssg_ref, wo_ref, l1g_ref, l1b_ref, w1_ref, w2_ref, l2g_ref, l2b_ref, o_ref):
    x = x_ref[...]
    o_b = _rms_groups(obf_ref[...] + obb_ref[...], DV_B) * dng_ref[...] * _silu(bg_ref[...])
    y_d = (ydf_ref[...] + ydb_ref[...]) * _silu(dz_ref[...])
    o_d = _rms_groups(y_d, GROUP_W // G_D) * ssg_ref[...]
    mixed = None
    for idx, part in enumerate((oa_ref[...], o_b, oc_ref[...], o_d)):
        y = _dot(part.astype(BF16), wo_ref[idx * GROUP_W:(idx + 1) * GROUP_W, :])
        mixed = y if mixed is None else mixed + y
    x1 = _layer_norm(ALPHA * x + mod_ref[2:3, :] * mixed, l1g_ref[...], l1b_ref[...])
    h16 = (x1 * (1.0 + mod_ref[4:5, :]) + mod_ref[3:4, :]).astype(BF16)
    ff = None
    fcol = 1024
    for j in range(D_FF // fcol):
        a = jnp.square(jnp.maximum(_dot(h16, w1_ref[:, j * fcol:(j + 1) * fcol]), 0.0)).astype(BF16)
        y = _dot(a, w2_ref[j * fcol:(j + 1) * fcol, :])
        ff = y if ff is None else ff + y
    o_ref[...] = _layer_norm(ALPHA * x1 + mod_ref[5:6, :] * ff, l2g_ref[...], l2b_ref[...])


def _outproj_ffn(x, mod, o_a, o_b2, b_gate, o_c, y_d2, d_z, wts, seq_len, per_batch_mod):
    t = x.shape[0]
    tb = 256
    nb_seq = seq_len // tb
    mod_idx = (lambda i: (1 + i // nb_seq, 0, 0)) if per_batch_mod else (lambda i: (0, 0, 0))
    row = lambda wd: pl.BlockSpec((tb, wd), lambda i: (i, 0))
    dir_spec = lambda d: pl.BlockSpec((None, tb, 256), lambda i: (d, i, 0))
    const = lambda i: (0, 0)
    single = pl.Buffered(1)
    in_specs = [row(D_MODEL), pl.BlockSpec((None, 6, D_MODEL), mod_idx),
                row(256), dir_spec(0), dir_spec(1), row(256), row(256), dir_spec(0), dir_spec(1), row(256),
                pl.BlockSpec((1, 256), const), pl.BlockSpec((1, 256), const),
                pl.BlockSpec((D_MODEL, D_MODEL), const, pipeline_mode=single),
                pl.BlockSpec((1, D_MODEL), const), pl.BlockSpec((1, D_MODEL), const),
                pl.BlockSpec((D_MODEL, D_FF), const, pipeline_mode=single),
                pl.BlockSpec((D_FF, D_MODEL), const, pipeline_mode=single),
                pl.BlockSpec((1, D_MODEL), const), pl.BlockSpec((1, D_MODEL), const)]
    return pl.pallas_call(
        _outproj_ffn_kernel,
        grid=(t // tb,),
        in_specs=in_specs,
        out_specs=row(D_MODEL),
        out_shape=jax.ShapeDtypeStruct((t, D_MODEL), F32),
        compiler_params=_cparams(1),
        name="outproj_ffn",
    )(x, mod, o_a, o_b2, o_b2, b_gate, o_c, y_d2, y_d2, d_z, *wts)


def _rope_tables(length, dim, width):
    rows = length // GRID_W
    row_pos = jnp.repeat(jnp.arange(rows, dtype=F32), GRID_W)
    col_pos = jnp.tile(jnp.arange(GRID_W, dtype=F32), rows)
    half = dim // 2
    inv_freq = ROPE_BASE ** (-jnp.arange(0, half, 2, dtype=F32) / half)
    ang_r = row_pos[:, None] * inv_freq
    ang_c = col_pos[:, None] * inv_freq
    ang = jnp.concatenate([ang_r, ang_r, ang_c, ang_c], axis=-1)
    cos, sin = jnp.cos(ang), jnp.sin(ang)
    first = (jnp.arange(dim) % (dim // 2)) < (dim // 4)
    sin_lo = jnp.where(first, -sin, 0.0)
    sin_hi = jnp.where(first, 0.0, sin)
    return cos, sin_lo, sin_hi


def _tile_lanes(tabs, reps):
    return tuple(jnp.tile(t, (1, reps)) for t in tabs)


def _pad_lanes(tabs, width):
    cos, lo, hi = tabs
    pad = width - cos.shape[1]
    return (jnp.pad(cos, ((0, 0), (0, pad)), constant_values=1.0), jnp.pad(lo, ((0, 0), (0, pad))),
            jnp.pad(hi, ((0, 0), (0, pad))))


def _permute_w_in(w_in):
    offs = {}
    o = 0
    for name, size in zip(SPLIT_NAMES, SPLIT_SIZES):
        offs[name] = (o, size)
        o += size
    cols = []
    for names, width in zip(OUT_GROUPS, OUT_WIDTHS):
        used = 0
        for n in names:
            s, size = offs[n]
            cols.append(w_in[:, :, s:s + size])
            used += size
        if used < width:
            cols.append(jnp.zeros(w_in.shape[:2] + (width - used,), w_in.dtype))
    return jnp.concatenate(cols, axis=-1).astype(BF16)


def _param_row(vals, col0, width=128, rows=8):
    out = jnp.zeros((DEPTH, rows, width), F32)
    for r, v in enumerate(vals):
        out = out.at[:, r, col0:col0 + v.shape[1]].set(v.astype(F32))
    return out


def kernel(x_prompt, x_sample, cache_diff_k, cache_diff_v, state_delta, cache_mla_ckv, cache_mla_krope, state_ssm, c, c_ctx, w_mod, b_mod, w_in, diff_lam, diff_norm, dn_conv, dn_a_log, dn_dt_bias, dn_norm, mla_q_norm, mla_kv_norm, mla_w_uq, mla_w_uk, mla_w_uv, ssm_conv_w, ssm_conv_b, ssm_a_log, ssm_dt_bias, ssm_d, ssm_norm, w_out, ln1_g, ln1_b, ln2_g, ln2_b, w_ff1, w_ff2):
    nb_p, len_p = x_prompt.shape[:2]
    nb_s, len_s = x_sample.shape[:2]
    past_len = cache_diff_k.shape[2]
    assert past_len == TILE and nb_s + 1 <= 8

    cvec = jnp.concatenate([c_ctx[None, :], c, jnp.zeros((8 - 1 - nb_s, D_MODEL), F32)], axis=0)
    mods = _mod_call(cvec, w_mod, b_mod).reshape(DEPTH, 8, 6, D_MODEL)

    w_in_p = _permute_w_in(w_in)
    w_uq = mla_w_uq.reshape(DEPTH, Q_RANK, H_C, NOPE_C + ROPE_C)
    w_uq_p = jnp.concatenate([w_uq[..., :NOPE_C].reshape(DEPTH, Q_RANK, H_C * NOPE_C),
                              w_uq[..., NOPE_C:].reshape(DEPTH, Q_RANK, H_C * ROPE_C)], axis=-1).astype(BF16)
    w_uk16, w_uv16 = mla_w_uk.astype(BF16), mla_w_uv.astype(BF16)
    w_out16, w_ff1_16, w_ff2_16 = w_out.astype(BF16), w_ff1.astype(BF16), w_ff2.astype(BF16)
    diff_g = jnp.tile(diff_norm, (1, H_A))[:, None, :]
    lam_init = jnp.asarray([0.8 - 0.6 * math.exp(-0.3 * l) for l in range(DEPTH)], F32)
    lam_rows = jnp.concatenate([diff_lam, jnp.broadcast_to(lam_init[:, None, None], (DEPTH, 4, DQK_A))], axis=1)
    dn_g = jnp.tile(dn_norm, (1, H_B))[:, None, :]
    ss_g = ssm_norm[:, None, :]
    dn_prm = _param_row([dn_a_log.reshape(DEPTH, 2 * H_B), dn_dt_bias.reshape(DEPTH, 2 * H_B)], COL_DECAY)
    ssm_prm = _param_row([ssm_a_log.reshape(DEPTH, 2 * H_D), ssm_dt_bias.reshape(DEPTH, 2 * H_D)], COL_DT, width=256)
    ssm_prm = ssm_prm.at[:, 2, :].set(jnp.repeat(ssm_d, P_D, axis=1))

    tabs_a = _tile_lanes(_rope_tables(len_s, DQK_A, 256), 256 // DQK_A)
    tabs_c = _rope_tables(len_s, ROPE_C, 128)
    tabs_small = _pad_lanes(tabs_c, 128)
    tabs_q = _tile_lanes(tabs_c, 128 // ROPE_C)

    past_kr = jnp.pad(cache_mla_krope, ((0, 0), (0, 0), (0, 0), (0, 128 - ROPE_C)))

    def layer(x, l, batch, seq_len, is_latent):
        mod = mods[l]
        (a_qkv, b_qkv, b_gate, c_q, d_z, d_xbc, c_kv, small) = _inproj(
            x, mod, w_in_p[l], seq_len, is_latent, tabs_a + tabs_small if is_latent else None)
        if is_latent:
            past_a = (cache_diff_k[:, l].reshape(batch, past_len, 256), cache_diff_v[:, l].reshape(batch, past_len, 256))
            past_c = (cache_mla_ckv[:, l], past_kr[:, l])
            s0_b, s0_d = state_delta[:, l], state_ssm[:, l]
        else:
            past_a = past_c = s0_b = s0_d = None
        o_a = _diff_attn(a_qkv, batch, seq_len, lam_rows[l], diff_g[l], past_a)
        o_c, ckv_n = _mla_attn(c_q, c_kv, small, batch, seq_len,
                               (mla_q_norm[l][None, :], mla_kv_norm[l][None, :], w_uq_p[l], w_uk16[l], w_uv16[l]),
                               past_c, tabs_q if is_latent else None)
        o_b2, st_b = _deltanet(b_qkv, small, batch, seq_len, dn_conv[l], dn_prm[l], s0_b)
        y_d2, st_d = _ssd(d_xbc, small, batch, seq_len, ssm_conv_w[l], ssm_conv_b[l][None, :], ssm_prm[l], s0_d)
        x = _outproj_ffn(x, mod, o_a, o_b2, b_gate, o_c, y_d2, d_z,
                         (dn_g[l], ss_g[l], w_out16[l], ln1_g[l][None, :], ln1_b[l][None, :], w_ff1_16[l], w_ff2_16[l],
                          ln2_g[l][None, :], ln2_b[l][None, :]), seq_len, is_latent)
        return x, (a_qkv, st_b, ckv_n, small, st_d)

    xp = x_prompt.reshape(nb_p * len_p, D_MODEL)
    xs = x_sample.reshape(nb_s * len_s, D_MODEL)
    ctx = ([], [], [], [], [], [])
    for l in range(DEPTH):
        xp, (a_qkv, st_b, ckv_n, small, st_d) = layer(xp, l, nb_p, len_p, False)
        xs, _ = layer(xs, l, nb_s, len_s, True)
        ctx[0].append(a_qkv[:, 256:512].reshape(nb_p, len_p, H_A, 2, DQK_A))
        ctx[1].append(a_qkv[:, 512:768].reshape(nb_p, len_p, H_A, DV_A))
        ctx[2].append(st_b)
        ctx[3].append(ckv_n.reshape(nb_p, len_p, KV_RANK))
        ctx[4].append(small[:, COL_KR:COL_KR + ROPE_C].reshape(nb_p, len_p, ROPE_C))
        ctx[5].append(st_d)
    return (xp.reshape(nb_p, len_p, D_MODEL), xs.reshape(nb_s, len_s, D_MODEL)) + tuple(
        jnp.stack(t, axis=1) for t in ctx)
```

```python
import functools
import math

import jax
import jax.numpy as jnp
import numpy as np
from jax import lax
from jax.experimental import pallas as pl
from jax.experimental.pallas import tpu as pltpu

F32 = jnp.float32
BF16 = jnp.bfloat16

D_MODEL = 1024
DEPTH = 4
GRID_W = 64
GROUP_W = 256
H_A, DV_A, DQK_A = 4, 64, 32
H_B, DK_B, DV_B = 4, 64, 64
H_C, NOPE_C, ROPE_C, V_C = 4, 64, 32, 64
Q_RANK, KV_RANK = 256, 128
H_D, P_D, N_D, G_D = 4, 64, 128, 2
CHUNK = 64
D_FF = 4 * D_MODEL
ROPE_BASE = 10000.0
EPS = 1e-6
ALPHA = (2 * DEPTH) ** 0.25
LOG2E = 1.4426950408889634

SPLIT_NAMES = ("a_q", "a_k", "a_v", "b_qkv", "b_beta", "b_decay", "b_gate",
               "c_q", "c_kv", "c_kr", "d_z", "d_xbc", "d_dt")
SPLIT_SIZES = (256, 256, 256, 768, 8, 8, 256, 256, 128, 32, 256, 768, 8)
OUT_GROUPS = (("a_q", "a_k", "a_v"), ("b_qkv",), ("b_gate",), ("c_q",), ("d_z",), ("d_xbc",),
              ("c_kv",), ("c_kr", "b_beta", "b_decay", "d_dt"))
OUT_WIDTHS = (768, 768, 256, 256, 256, 768, 128, 128)
IN_COLS_PAD = sum(OUT_WIDTHS)
COL_KR, COL_BETA, COL_DECAY, COL_DT = 0, 32, 40, 48

TILE = 256
TQ = 256
TB = 512
VMEM_LIMIT = 56 * 1024 * 1024


def _cparams(n_axes):
    return pltpu.CompilerParams(dimension_semantics=("arbitrary",) * n_axes,
                                vmem_limit_bytes=VMEM_LIMIT)


def _dot(a, b):
    return jnp.dot(a, b, preferred_element_type=F32)


def _dot_nt(a, b):
    return lax.dot_general(a, b, (((1,), (1,)), ((), ())), preferred_element_type=F32)


def _split2(x):
    hi = x.astype(BF16)
    return hi, (x - hi.astype(F32)).astype(BF16)


def _sigmoid(x):
    return 1.0 / (1.0 + jnp.exp(-x))


def _silu(x):
    return x * _sigmoid(x)


def _softplus(x):
    return jnp.maximum(x, 0.0) + jnp.log1p(jnp.exp(-jnp.abs(x)))


def _rms_groups(x, width):
    outs = []
    for s in range(0, x.shape[1], width):
        xg = x[:, s:s + width]
        ms = jnp.mean(xg * xg, axis=-1, keepdims=True)
        outs.append(xg * lax.rsqrt(ms + EPS))
    return outs[0] if len(outs) == 1 else jnp.concatenate(outs, axis=1)


def _layer_norm(x, g, b):
    mu = jnp.mean(x, axis=-1, keepdims=True)
    xc = x - mu
    var = jnp.mean(xc * xc, axis=-1, keepdims=True)
    return xc * lax.rsqrt(var + EPS) * g + b


def _rope(x, cos, sin_lo, sin_hi):
    w = x.shape[1]
    return x * cos + pltpu.roll(x, w - 8, 1) * sin_lo + pltpu.roll(x, 8, 1) * sin_hi


def _mod_kernel(c_ref, w_ref, b_ref, o_ref):
    c = c_ref[...]
    o_ref[...] = _dot(_silu(c).astype(BF16), w_ref[...].astype(BF16)) + b_ref[...]


def _mod_call(cvec, w_mod, b_mod):
    nblk = 4
    wcol = 6 * D_MODEL // nblk
    return pl.pallas_call(
        _mod_kernel,
        grid=(DEPTH, nblk),
        in_specs=[pl.BlockSpec((8, D_MODEL), lambda l, j: (0, 0)),
                  pl.BlockSpec((None, D_MODEL, wcol), lambda l, j: (l, 0, j)),
                  pl.BlockSpec((None, 1, wcol), lambda l, j: (l, 0, j))],
        out_specs=pl.BlockSpec((None, 8, wcol), lambda l, j: (l, 0, j)),
        out_shape=jax.ShapeDtypeStruct((DEPTH, 8, 6 * D_MODEL), F32),
        compiler_params=_cparams(2),
        name="mod",
    )(cvec, w_mod, b_mod.reshape(DEPTH, 1, 6 * D_MODEL))


def _inproj_kernel(has_rope, *refs):
    if has_rope:
        (x_ref, mod_ref, w_ref, cos_ref, slo_ref, shi_ref, cos_s_ref, slo_s_ref, shi_s_ref,
         oa, obq, obg, ocq, odz, odx, ockv, osm) = refs
    else:
        x_ref, mod_ref, w_ref, oa, obq, obg, ocq, odz, odx, ockv, osm = refs
    x = x_ref[...]
    h = (x * (1.0 + mod_ref[1:2, :]) + mod_ref[0:1, :]).astype(BF16)
    outs = (oa, obq, obg, ocq, odz, odx, ockv, osm)
    off = 0
    for idx, (o_ref, wd) in enumerate(zip(outs, OUT_WIDTHS)):
        y = _dot(h, w_ref[:, off:off + wd])
        off += wd
        if has_rope and idx == 0:
            cos, slo, shi = cos_ref[...], slo_ref[...], shi_ref[...]
            o_ref[:, 0:256] = _rope(y[:, 0:256], cos, slo, shi)
            o_ref[:, 256:512] = _rope(y[:, 256:512], cos, slo, shi)
            o_ref[:, 512:768] = y[:, 512:768]
        elif has_rope and idx == 7:
            o_ref[...] = _rope(y, cos_s_ref[...], slo_s_ref[...], shi_s_ref[...])
        else:
            o_ref[...] = y


def _inproj(x, mod, w_in_p, seq_len, per_batch_mod, rope_tabs):
    t = x.shape[0]
    nb_seq = seq_len // TB
    has_rope = rope_tabs is not None
    mod_idx = (lambda i: (1 + i // nb_seq, 0, 0)) if per_batch_mod else (lambda i: (0, 0, 0))
    in_specs = [pl.BlockSpec((TB, D_MODEL), lambda i: (i, 0)),
                pl.BlockSpec((None, 6, D_MODEL), mod_idx),
                pl.BlockSpec((D_MODEL, IN_COLS_PAD), lambda i: (0, 0))]
    args = [x, mod, w_in_p]
    if has_rope:
        for tab in rope_tabs:
            in_specs.append(pl.BlockSpec((TB, tab.shape[1]), lambda i: (i % nb_seq, 0)))
            args.append(tab)
    return pl.pallas_call(
        functools.partial(_inproj_kernel, has_rope),
        grid=(t // TB,),
        in_specs=in_specs,
        out_specs=[pl.BlockSpec((TB, wd), lambda i: (i, 0)) for wd in OUT_WIDTHS],
        out_shape=[jax.ShapeDtypeStruct((t, wd), F32) for wd in OUT_WIDTHS],
        compiler_params=_cparams(1),
        name="inproj",
    )(*args)


def _softmax_probs(q16, kt_ref, k_row0):
    s = _dot(q16, kt_ref[k_row0:k_row0 + q16.shape[1], :])
    return jnp.exp2(s - jnp.max(s, axis=-1, keepdims=True)).astype(BF16)


def _with_ones(v):
    return jnp.concatenate([v, jnp.ones((v.shape[0], 128 - v.shape[1]), v.dtype)], axis=1).astype(BF16)


def _diff_attn_kernel(has_past, nch_own, tq, *refs):
    if has_past:
        q_ref, k_ref, v_ref, pk_ref, pv_ref, lam_ref, g_ref, o_ref, kt_scr, v_scr = refs
    else:
        q_ref, k_ref, v_ref, lam_ref, g_ref, o_ref, kt_scr, v_scr = refs

    @pl.when(pl.program_id(1) == 0)
    def _prep():
        def put(c, k, v):
            cols = slice(c * TILE, (c + 1) * TILE)
            kt_scr[:, cols] = k.T.astype(BF16)
            for h in range(H_A):
                v_scr[h, cols, :] = _with_ones(v[:, h * DV_A:(h + 1) * DV_A])

        for c in range(nch_own):
            put(c, k_ref[c * TILE:(c + 1) * TILE, :], v_ref[c * TILE:(c + 1) * TILE, :])
        if has_past:
            put(nch_own, pk_ref[...], pv_ref[...])

    lp = lam_ref[...]
    lam_init = lam_ref[4:5, 0:1]
    lam = (jnp.exp(jnp.sum(lp[0:1] * lp[1:2], axis=-1, keepdims=True))
           - jnp.exp(jnp.sum(lp[2:3] * lp[3:4], axis=-1, keepdims=True)) + lam_init)
    q = (q_ref[...] * (DQK_A ** -0.5 * LOG2E)).astype(BF16)
    heads = []
    for h in range(H_A):
        c0 = h * 2 * DQK_A
        p = jnp.concatenate([_softmax_probs(q[:, c0:c0 + DQK_A], kt_scr, c0),
                             _softmax_probs(q[:, c0 + DQK_A:c0 + 2 * DQK_A], kt_scr, c0 + DQK_A)], axis=0)
        acc = _dot(p, v_scr[h])
        o1 = acc[:tq, :DV_A] / acc[:tq, DV_A:DV_A + 1]
        o2 = acc[tq:, :DV_A] / acc[tq:, DV_A:DV_A + 1]
        heads.append(o1 - lam * o2)
    o = jnp.concatenate(heads, axis=1)
    o_ref[...] = _rms_groups(o, DV_A) * g_ref[...] * (1.0 - lam_init)


def _diff_attn(a_qkv, batch, seq_len, lam_p, g_row, past):
    tq = min(TQ, seq_len)
    nq = seq_len // tq
    nch_own = seq_len // TILE
    has_past = past is not None
    n_keys = seq_len + (TILE if has_past else 0)
    in_specs = [pl.BlockSpec((tq, 256), lambda b, i: (b * nq + i, 0)),
                pl.BlockSpec((seq_len, 256), lambda b, i: (b, 1)),
                pl.BlockSpec((seq_len, 256), lambda b, i: (b, 2))]
    args = [a_qkv, a_qkv, a_qkv]
    if has_past:
        in_specs += [pl.BlockSpec((None, TILE, 256), lambda b, i: (b, 0, 0))] * 2
        args += list(past)
    in_specs += [pl.BlockSpec((8, DQK_A), lambda b, i: (0, 0)), pl.BlockSpec((1, 256), lambda b, i: (0, 0))]
    args += [lam_p, g_row]
    return pl.pallas_call(
        functools.partial(_diff_attn_kernel, has_past, nch_own, tq),
        grid=(batch, nq),
        in_specs=in_specs,
        out_specs=pl.BlockSpec((tq, 256), lambda b, i: (b * nq + i, 0)),
        out_shape=jax.ShapeDtypeStruct((batch * seq_len, 256), F32),
        scratch_shapes=[pltpu.VMEM((256, n_keys), BF16), pltpu.VMEM((H_A, n_keys, 128), BF16)],
        compiler_params=_cparams(2),
        name="diff_attn",
    )(*args)


def _mla_kernel(has_past, has_rope, nch_own, tq, *refs):
    refs = list(refs)
    cq_ref, ckv_ref, sm_ref = refs[:3]
    pos = 3
    if has_past:
        pckv_ref, pkr_ref = refs[pos:pos + 2]
        pos += 2
    qn_ref, kvn_ref, wuq_ref, wuk_ref, wuv_ref = refs[pos:pos + 5]
    pos += 5
    if has_rope:
        cos_ref, slo_ref, shi_ref = refs[pos:pos + 3]
        pos += 3
    o_ref, ckv_out = refs[pos:pos + 2]
    kt_scr, v_scr = refs[pos + 2:]

    def put(c, ckv, kr):
        cols = slice(c * TILE, (c + 1) * TILE)
        c16 = ckv.astype(BF16)
        knt = _dot(c16, wuk_ref[...]).T
        v = _dot(c16, wuv_ref[...])
        lane = lax.broadcasted_iota(jnp.int32, kr.shape, 1)
        krt = jnp.where(lane < ROPE_C, kr, 0.0).T[0:64, :].astype(BF16)
        for h in range(H_C):
            kt_scr[h * 128:h * 128 + NOPE_C, cols] = knt[h * NOPE_C:(h + 1) * NOPE_C, :].astype(BF16)
            kt_scr[h * 128 + NOPE_C:(h + 1) * 128, cols] = krt
            v_scr[h, cols, :] = _with_ones(v[:, h * V_C:(h + 1) * V_C])

    @pl.when(pl.program_id(1) == 0)
    def _prep():
        for c in range(nch_own):
            rows = slice(c * TILE, (c + 1) * TILE)
            ckv = _rms_groups(ckv_ref[rows, :], KV_RANK) * kvn_ref[...]
            ckv_out[rows, :] = ckv
            put(c, ckv, sm_ref[rows, :])
        if has_past:
            put(nch_own, pckv_ref[...], pkr_ref[...])

    cqn = (_rms_groups(cq_ref[...], Q_RANK) * qn_ref[...]).astype(BF16)
    qa = _dot(cqn, wuq_ref[...])
    q_nope = qa[:, 0:256]
    q_rope = qa[:, 256:384]
    if has_rope:
        q_rope = _rope(q_rope, cos_ref[...], slo_ref[...], shi_ref[...])
    scale = (NOPE_C + ROPE_C) ** -0.5 * LOG2E
    zpad = jnp.zeros((tq, 128 - NOPE_C - ROPE_C), F32)
    heads = []
    for h in range(H_C):
        qh = jnp.concatenate([q_nope[:, h * NOPE_C:(h + 1) * NOPE_C],
                              q_rope[:, h * ROPE_C:(h + 1) * ROPE_C], zpad], axis=1) * scale
        acc = _dot(_softmax_probs(qh.astype(BF16), kt_scr, h * 128), v_scr[h])
        heads.append(acc[:, :V_C] / acc[:, V_C:V_C + 1])
    o_ref[...] = jnp.concatenate(heads, axis=1)


def _mla_attn(c_q, c_kv, small, batch, seq_len, wts, past, rope_tabs):
    tq = min(TQ, seq_len)
    nq = seq_len // tq
    nch_own = seq_len // TILE
    has_past = past is not None
    has_rope = rope_tabs is not None
    n_keys = seq_len + (TILE if has_past else 0)
    const = lambda b, i: (0, 0)
    in_specs = [pl.BlockSpec((tq, Q_RANK), lambda b, i: (b * nq + i, 0)),
                pl.BlockSpec((seq_len, KV_RANK), lambda b, i: (b, 0)),
                pl.BlockSpec((seq_len, 128), lambda b, i: (b, 0))]
    args = [c_q, c_kv, small]
    if has_past:
        in_specs += [pl.BlockSpec((None, TILE, 128), lambda b, i: (b, 0, 0))] * 2
        args += list(past)
    in_specs += [pl.BlockSpec((1, Q_RANK), const), pl.BlockSpec((1, KV_RANK), const),
                 pl.BlockSpec((Q_RANK, 384), const), pl.BlockSpec((KV_RANK, 256), const),
                 pl.BlockSpec((KV_RANK, 256), const)]
    args += list(wts)
    if has_rope:
        in_specs += [pl.BlockSpec((tq, 128), lambda b, i: (i, 0))] * 3
        args += list(rope_tabs)
    return pl.pallas_call(
        functools.partial(_mla_kernel, has_past, has_rope, nch_own, tq),
        grid=(batch, nq),
        in_specs=in_specs,
        out_specs=[pl.BlockSpec((tq, 256), lambda b, i: (b * nq + i, 0)),
                   pl.BlockSpec((seq_len, KV_RANK), lambda b, i: (b, 0))],
        out_shape=[jax.ShapeDtypeStruct((batch * seq_len, 256), F32),
                   jax.ShapeDtypeStruct((batch * seq_len, KV_RANK), F32)],
        scratch_shapes=[pltpu.VMEM((512, n_keys), BF16), pltpu.VMEM((H_C, n_keys, 128), BF16)],
        compiler_params=_cparams(2),
        name="mla_attn",
    )(*args)


def _chunk_masks(reverse):
    i = lax.broadcasted_iota(jnp.int32, (TILE, TILE), 0)
    j = lax.broadcasted_iota(jnp.int32, (TILE, TILE), 1)
    same = (i >> 6) == (j >> 6)
    d = (j - i) if reverse else (i - j)
    causal = jnp.logical_and(same, d >= 0)
    strict = jnp.logical_and(same, d > 0)
    return same, causal, strict


def _chunk_sums(xt, same, causal):
    m_run = jnp.where(causal, 1.0, 0.0).astype(BF16)
    m_all = jnp.where(same, 1.0, 0.0).astype(BF16)
    hi, lo = _split2(xt)
    run_r = _dot_nt(hi, m_run) + _dot_nt(lo, m_run)
    tot_r = _dot_nt(hi, m_all) + _dot_nt(lo, m_all)
    return run_r.T, run_r, tot_r.T


def _conv3(main, prev8, next8, w, has_prev, has_next):
    n = main.shape[0]
    row = lax.broadcasted_iota(jnp.int32, main.shape, 0)
    before = jnp.where(has_prev, prev8[7:8, :], 0.0)
    after = jnp.where(has_next, next8[0:1, :], 0.0)
    xm = jnp.where(row == 0, before, pltpu.roll(main, 1, 0))
    xp = jnp.where(row == n - 1, after, pltpu.roll(main, n - 1, 0))
    return xm * w[0:1, :] + main * w[1:2, :] + xp * w[2:3, :]


def _level_masks():
    i = lax.broadcasted_iota(jnp.int32, (TILE, TILE), 0)
    j = lax.broadcasted_iota(jnp.int32, (TILE, TILE), 1)
    return [jnp.logical_and((i >> (lvl + 1)) == (j >> (lvl + 1)), (i >> lvl) != (j >> lvl))
            for lvl in range(CHUNK.bit_length() - 1)]


def _scan_specs(batch, seq_len, width):
    nt = seq_len // TILE
    last8 = batch * seq_len // 8 - 1

    def tile(b, d, i):
        return b * nt + i + d * (nt - 1 - 2 * i)

    main = pl.BlockSpec((TILE, width), lambda b, d, i: (tile(b, d, i), 0))
    prev = pl.BlockSpec((8, width), lambda b, d, i: (jnp.maximum(tile(b, d, i) * (TILE // 8) - 1, 0), 0))
    nxt = pl.BlockSpec((8, width), lambda b, d, i: (jnp.minimum((tile(b, d, i) + 1) * (TILE // 8), last8), 0))
    return main, prev, nxt, tile


def _tile_specs(batch, seq_len, widths, reverse):
    nt = seq_len // TILE
    last8 = batch * seq_len // 8 - 1

    def tile(b, i):
        return b * nt + ((nt - 1 - i) if reverse else i)

    specs = []
    for width, halos in widths:
        specs.append(pl.BlockSpec((TILE, width), lambda b, i: (tile(b, i), 0)))
        if halos:
            specs.append(pl.BlockSpec((8, width), lambda b, i: (jnp.maximum(tile(b, i) * (TILE // 8) - 1, 0), 0)))
            specs.append(pl.BlockSpec((8, width), lambda b, i: (jnp.minimum((tile(b, i) + 1) * (TILE // 8), last8), 0)))
    return specs


def _deltanet_kernel(nt, has_state, *refs):
    refs = list(refs)
    tiles = (refs[0:4], refs[4:8])
    w_ref, prm_ref, cf_ref, cb_ref, ce_ref = refs[8:13]
    n_in = 14 if has_state else 13
    of_ref, ob_ref, sfin_ref, s_scr = refs[n_in:]
    step = pl.program_id(1)
    n_chunks = TILE // CHUNK

    @pl.when(step == 0)
    def _init():
        s_scr[...] = jnp.zeros(s_scr.shape, F32)
        if has_state:
            for d in range(2):
                for h in range(H_B):
                    s_scr[d, h * DK_B:(h + 1) * DK_B, h * DV_B:(h + 1) * DV_B] = refs[13][d, h]

    block16 = cb_ref[CB_SAME]
    block = cf_ref[CF_SAME]

    def expand(x, e16):
        hi, lo = _split2(x)
        return _dot(hi, e16) + _dot(lo, e16)

    def head_sums(x):
        hi, lo = _split2(x)
        return _dot(hi, block16) + _dot(lo, block16)

    chains, dirs = [], []
    shared = None
    for dirn in (0, 1):
        main_ref, prev_ref, next_ref, sm_ref = tiles[dirn]
        tpos = (nt - 1 - step) if dirn == 1 else step
        if shared is None or nt > 1:
            qkv = _silu(_conv3(main_ref[...], prev_ref[...], next_ref[...], w_ref[...], tpos > 0, tpos < nt - 1))
            q, k, v = qkv[:, 0:256], qkv[:, 256:512], qkv[:, 512:768]
            qn = q * lax.rsqrt(head_sums(q * q) + EPS) * (DK_B ** -0.5)
            kn = k * lax.rsqrt(head_sums(k * k) + EPS)
            knt16 = kn.T.astype(BF16)
            sm = sm_ref[...]
            lane = lax.broadcasted_iota(jnp.int32, sm.shape, 1)
            beta = _sigmoid(sm)
            g = -jnp.exp(prm_ref[0:1, :]) * _softplus(sm + prm_ref[1:2, :])
            g = jnp.where(jnp.logical_and(lane >= COL_DECAY, lane < COL_DECAY + 2 * H_B), g, 0.0)
            shared = (qn, kn, knt16, v, beta, g)
        qn, kn, knt16, v, beta, g = shared
        beta_x = expand(beta, ce_ref[dirn])
        g_hi, g_lo = _split2(expand(g, ce_ref[2 + dirn]))
        gc = _dot(cb_ref[dirn], g_hi) + _dot(cb_ref[dirn], g_lo)
        gt = _dot(block16, g_hi) + _dot(block16, g_lo)
        gc_t = gc.T
        egc = jnp.exp(gc)
        kb = kn * beta_x
        dirs.append(dict(vb=v * beta_x, kbe=kb * egc, qd=qn * egc, egl=jnp.exp(gt),
                         kdt=(kn * jnp.exp(gt - gc)).T.astype(BF16), s=s_scr[dirn], o=[None] * n_chunks))
        kb16, qn16 = kb.astype(BF16), qn.astype(BF16)
        for h in range(H_B):
            lanes16 = cb_ref[CB_LANES + h]
            dec = jnp.exp((gc[:, h * DK_B:h * DK_B + 1] - gc_t[h * DK_B:h * DK_B + 1, :]) + cf_ref[dirn])
            a = _dot(kb16 * lanes16, knt16) * dec * cf_ref[CF_OFFDIAG]
            chains.append(dict(dirn=dirn, h=h, a=a, r=-(a * cf_ref[CF_LEVEL]),
                               qk16=(_dot(qn16 * lanes16, knt16) * dec).astype(BF16)))

    for lvl in range(1, CHUNK.bit_length() - 1):
        for ch in chains:
            e = ch["a"] * cf_ref[CF_LEVEL + lvl]
            x = e + _dot(ch["r"].astype(BF16), e.astype(BF16))
            ch["r"] = ch["r"] - x - _dot(x.astype(BF16), ch["r"].astype(BF16))

    for dirn, dd in enumerate(dirs):
        mine = [ch for ch in chains if ch["dirn"] == dirn]
        vb16, kbe16 = dd["vb"].astype(BF16), dd["kbe"].astype(BF16)
        u, w = dd["vb"], dd["kbe"]
        for ch in mine:
            r16, lanes16 = ch["r"].astype(BF16), cb_ref[CB_LANES + ch["h"]]
            u = u + _dot(r16, vb16 * lanes16)
            w = w + _dot(r16, kbe16 * lanes16)
        u16, w16 = u.astype(BF16), w.astype(BF16)
        qku, qkw = None, None
        for ch in mine:
            lanes16 = cb_ref[CB_LANES + ch["h"]]
            a_u, a_w = _dot(ch["qk16"], u16 * lanes16), _dot(ch["qk16"], w16 * lanes16)
            qku = a_u if qku is None else qku + a_u
            qkw = a_w if qkw is None else qkw + a_w
        dd["qku"] = qku
        dd["qp16"] = (dd["qd"] - qkw).astype(BF16)
        dd["wu16"] = jnp.concatenate([w16, u16], axis=1)

    for ci in range(n_chunks):
        for dirn, dd in enumerate(dirs):
            c = (n_chunks - 1 - ci) if dirn == 1 else ci
            rows = slice(c * CHUNK, (c + 1) * CHUNK)
            s16 = dd["s"].astype(BF16)
            dd["o"][c] = _dot(dd["qp16"][rows], s16) + dd["qku"][rows]
            pb = _dot(dd["kdt"] * cb_ref[CB_LANES + c], dd["wu16"])
            p16 = (pb[:, 0:256] * block).astype(BF16)
            dd["s"] = dd["s"] * dd["egl"][c * CHUNK:c * CHUNK + 1, :] - _dot(p16, s16) + pb[:, 256:512] * block

    for dirn, (dd, o_ref) in enumerate(zip(dirs, (of_ref, ob_ref))):
        s_scr[dirn] = dd["s"]
        o_ref[...] = jnp.concatenate(dd["o"], axis=0)

    @pl.when(step == nt - 1)
    def _fin():
        for d in range(2):
            for h in range(H_B):
                sfin_ref[d, h] = s_scr[d, h * DK_B:(h + 1) * DK_B, h * DV_B:(h + 1) * DV_B]


CF_NINF, CF_OFFDIAG, CF_LEVEL, CF_SAME = 0, 2, 3, 9
CB_RUN, CB_SAME, CB_LANES = 0, 2, 3


def _deltanet_consts():
    i = np.arange(TILE)[:, None]
    j = np.arange(TILE)[None, :]
    same = (i // CHUNK) == (j // CHUNK)
    run = [same & (i >= j), same & (i <= j)]
    levels = [((i >> (l + 1)) == (j >> (l + 1))) & ((i >> l) != (j >> l)) for l in range(CHUNK.bit_length() - 1)]
    cf = np.stack([np.where(m, 0.0, -np.inf) for m in run] + [i != j] + levels + [same]).astype(np.float32)
    cb = np.stack(run + [same] + [np.broadcast_to(j // CHUNK == c, (TILE, TILE)) for c in range(TILE // CHUNK)])
    rows = np.arange(128)[:, None]
    ce = np.stack([rows == col0 + d * H_B + j // DK_B for col0 in (COL_BETA, COL_DECAY) for d in range(2)])
    return jnp.asarray(cf), jnp.asarray(cb, BF16), jnp.asarray(ce, BF16)


def _deltanet(b_qkv, small, batch, seq_len, conv_w, prm, s0):
    nt = seq_len // TILE
    has_state = s0 is not None
    const = lambda b, i: (0, 0)
    const3 = lambda b, i: (0, 0, 0)
    consts = _deltanet_consts()
    widths = ((768, True), (128, False))
    in_specs = (_tile_specs(batch, seq_len, widths, False) + _tile_specs(batch, seq_len, widths, True)
                + [pl.BlockSpec((3, 768), const), pl.BlockSpec((8, 128), const)]
                + [pl.BlockSpec(cst.shape, const3) for cst in consts])
    args = [b_qkv, b_qkv, b_qkv, small] * 2 + [conv_w, prm] + list(consts)
    state_spec = pl.BlockSpec((None, 2, H_B, DK_B, DV_B), lambda b, i: (b, 0, 0, 0, 0))
    if has_state:
        in_specs.append(state_spec)
        args.append(s0)
    return pl.pallas_call(
        functools.partial(_deltanet_kernel, nt, has_state),
        grid=(batch, nt),
        in_specs=in_specs,
        out_specs=[pl.BlockSpec((TILE, 256), lambda b, i: (b * nt + i, 0)),
                   pl.BlockSpec((TILE, 256), lambda b, i: (b * nt + nt - 1 - i, 0)), state_spec],
        out_shape=[jax.ShapeDtypeStruct((batch * seq_len, 256), F32)] * 2
        + [jax.ShapeDtypeStruct((batch, 2, H_B, DK_B, DV_B), F32)],
        scratch_shapes=[pltpu.VMEM((2, H_B * DK_B, H_B * DV_B), F32)],
        compiler_params=_cparams(2),
        name="deltanet",
    )(*args)


def _ssd_body(reverse, nt, has_state, refs):
    if has_state:
        main_ref, prev_ref, next_ref, sm_ref, w_ref, cb_ref, prm_ref, s0_ref, y_ref, sfin_ref, s_scr = refs
    else:
        main_ref, prev_ref, next_ref, sm_ref, w_ref, cb_ref, prm_ref, y_ref, sfin_ref, s_scr = refs
    step = pl.program_id(2)
    pos = (nt - 1 - step) if reverse else step
    dirn = 1 if reverse else 0

    @pl.when(step == 0)
    def _init():
        if has_state:
            s_scr[...] = s0_ref[...]
        else:
            s_scr[...] = jnp.zeros(s_scr.shape, F32)

    xbc = _silu(_conv3(main_ref[...], prev_ref[...], next_ref[...], w_ref[...], pos > 0, pos < nt - 1) + cb_ref[...])
    x_all = xbc[:, 0:256]
    sm = sm_ref[...]
    lane = lax.broadcasted_iota(jnp.int32, sm.shape, 1)
    dt = _softplus(sm + prm_ref[1:2, 0:128])
    da = jnp.where(jnp.logical_and(lane >= COL_DT, lane < COL_DT + 2 * H_D),
                   dt * (-jnp.exp(prm_ref[0:1, 0:128])), 0.0)
    same, causal, _ = _chunk_masks(reverse)
    ac_c, ac_r, at_c = _chunk_sums(da.T, same, causal)

    xdts = []
    for h in range(H_D):
        col = COL_DT + dirn * H_D + h
        xdts.append(x_all[:, h * P_D:(h + 1) * P_D] * dt[:, col:col + 1])
    xdt_t = jnp.concatenate(xdts, axis=1).T.astype(BF16)
    row_chunk = lax.broadcasted_iota(jnp.int32, (TILE, N_D), 0) >> 6

    outs = []
    cb = None
    for h in range(H_D):
        grp = h // (H_D // G_D)
        bm = xbc[:, 256 + grp * N_D:256 + (grp + 1) * N_D]
        cm = xbc[:, 512 + grp * N_D:512 + (grp + 1) * N_D]
        if h % (H_D // G_D) == 0:
            cb = _dot_nt(cm.astype(BF16), bm.astype(BF16))
        col = COL_DT + dirn * H_D + h
        ac = ac_c[:, col:col + 1]
        seg = jnp.exp(jnp.where(causal, ac - ac_r[col:col + 1, :], -jnp.inf))
        xdt16 = xdts[h].astype(BF16)
        y = _dot((cb * seg).astype(BF16), xdt16)
        b_dec = bm * jnp.exp(at_c[:, col:col + 1] - ac)
        cd16 = (cm * jnp.exp(ac)).astype(BF16)
        elast = jnp.exp(at_c[:, col:col + 1])
        s = s_scr[h]
        y_rows = [None] * 4
        for c in (range(3, -1, -1) if reverse else range(4)):
            r = slice(c * CHUNK, (c + 1) * CHUNK)
            y_rows[c] = _dot_nt(cd16[r], s.astype(BF16))
            st = _dot(xdt_t[h * P_D:(h + 1) * P_D, :], jnp.where(row_chunk == c, b_dec, 0.0).astype(BF16))
            s = s * elast[c * CHUNK:c * CHUNK + 1, :] + st
        s_scr[h] = s
        y = y + jnp.concatenate(y_rows, axis=0)
        if not reverse:
            y = y + prm_ref[2:3, h * P_D:(h + 1) * P_D] * x_all[:, h * P_D:(h + 1) * P_D]
        outs.append(y)
    y_ref[...] = jnp.concatenate(outs, axis=1)

    @pl.when(step == nt - 1)
    def _fin():
        sfin_ref[...] = s_scr[...]


def _ssd_kernel(nt, has_state, *refs):
    d = pl.program_id(1)

    @pl.when(d == 0)
    def _fwd():
        _ssd_body(False, nt, has_state, refs)

    @pl.when(d == 1)
    def _bwd():
        _ssd_body(True, nt, has_state, refs)


def _ssd(d_xbc, small, batch, seq_len, conv_w, conv_b, prm, s0):
    nt = seq_len // TILE
    has_state = s0 is not None
    main, prev, nxt, tile = _scan_specs(batch, seq_len, 768)
    const = lambda b, d, i: (0, 0)
    in_specs = [main, prev, nxt, pl.BlockSpec((TILE, 128), lambda b, d, i: (tile(b, d, i), 0)),
                pl.BlockSpec((3, 768), const), pl.BlockSpec((1, 768), const), pl.BlockSpec((8, 256), const)]
    args = [d_xbc, d_xbc, d_xbc, small, conv_w, conv_b, prm]
    if has_state:
        in_specs.append(pl.BlockSpec((None, None, H_D, P_D, N_D), lambda b, d, i: (b, d, 0, 0, 0)))
        args.append(s0)
    return pl.pallas_call(
        functools.partial(_ssd_kernel, nt, has_state),
        grid=(batch, 2, nt),
        in_specs=in_specs,
        out_specs=[pl.BlockSpec((None, TILE, 256), lambda b, d, i: (d, tile(b, d, i), 0)),
                   pl.BlockSpec((None, None, H_D, P_D, N_D), lambda b, d, i: (b, d, 0, 0, 0))],
        out_shape=[jax.ShapeDtypeStruct((2, batch * seq_len, 256), F32),
                   jax.ShapeDtypeStruct((batch, 2, H_D, P_D, N_D), F32)],
        scratch_shapes=[pltpu.VMEM((H_D, P_D, N_D), F32)],
        compiler_params=_cparams(3),
        name="ssd",
    )(*args)


def _outproj_ffn_kernel(x_ref, mod_ref, oa_ref, obf_ref, obb_ref, bg_ref, oc_ref, ydf_ref, ydb_ref, dz_ref,
                        dng_ref, ssg_ref, wo_ref, l1g_ref, l1b_ref, w1_ref, w2_ref, l2g_ref, l2b_ref, o_ref):
    x = x_ref[...]
    o_b = _rms_groups(obf_ref[...] + obb_ref[...], DV_B) * dng_ref[...] * _silu(bg_ref[...])
    y_d = (ydf_ref[...] + ydb_ref[...]) * _silu(dz_ref[...])
    o_d = _rms_groups(y_d, GROUP_W // G_D) * ssg_ref[...]
    mixed = None
    for idx, part in enumerate((oa_ref[...], o_b, oc_ref[...], o_d)):
        y = _dot(part.astype(BF16), wo_ref[idx * GROUP_W:(idx + 1) * GROUP_W, :])
        mixed = y if mixed is None else mixed + y
    x1 = _layer_norm(ALPHA * x + mod_ref[2:3, :] * mixed, l1g_ref[...], l1b_ref[...])
    h16 = (x1 * (1.0 + mod_ref[4:5, :]) + mod_ref[3:4, :]).astype(BF16)
    ff = None
    fcol = 1024
    for j in range(D_FF // fcol):
        a = jnp.square(jnp.maximum(_dot(h16, w1_ref[:, j * fcol:(j + 1) * fcol]), 0.0)).astype(BF16)
        y = _dot(a, w2_ref[j * fcol:(j + 1) * fcol, :])
        ff = y if ff is None else ff + y
    o_ref[...] = _layer_norm(ALPHA * x1 + mod_ref[5:6, :] * ff, l2g_ref[...], l2b_ref[...])


def _outproj_ffn(x, mod, o_a, o_bf, o_bb, b_gate, o_c, y_d2, d_z, wts, seq_len, per_batch_mod):
    t = x.shape[0]
    tb = min(TB, seq_len)
    nb_seq = seq_len // tb
    mod_idx = (lambda i: (1 + i // nb_seq, 0, 0)) if per_batch_mod else (lambda i: (0, 0, 0))
    row = lambda wd: pl.BlockSpec((tb, wd), lambda i: (i, 0))
    dir_spec = lambda d: pl.BlockSpec((None, tb, 256), lambda i: (d, i, 0))
    const = lambda i: (0, 0)
    single = pl.Buffered(1)
    in_specs = [row(D_MODEL), pl.BlockSpec((None, 6, D_MODEL), mod_idx),
                row(256), row(256), row(256), row(256), row(256), dir_spec(0), dir_spec(1), row(256),
                pl.BlockSpec((1, 256), const), pl.BlockSpec((1, 256), const),
                pl.BlockSpec((D_MODEL, D_MODEL), const, pipeline_mode=single),
                pl.BlockSpec((1, D_MODEL), const), pl.BlockSpec((1, D_MODEL), const),
                pl.BlockSpec((D_MODEL, D_FF), const, pipeline_mode=single),
                pl.BlockSpec((D_FF, D_MODEL), const, pipeline_mode=single),
                pl.BlockSpec((1, D_MODEL), const), pl.BlockSpec((1, D_MODEL), const)]
    return pl.pallas_call(
        _outproj_ffn_kernel,
        grid=(t // tb,),
        in_specs=in_specs,
        out_specs=row(D_MODEL),
        out_shape=jax.ShapeDtypeStruct((t, D_MODEL), F32),
        compiler_params=_cparams(1),
        name="outproj_ffn",
    )(x, mod, o_a, o_bf, o_bb, b_gate, o_c, y_d2, y_d2, d_z, *wts)


def _rope_tables(length, dim, width):
    rows = length // GRID_W
    row_pos = jnp.repeat(jnp.arange(rows, dtype=F32), GRID_W)
    col_pos = jnp.tile(jnp.arange(GRID_W, dtype=F32), rows)
    half = dim // 2
    inv_freq = ROPE_BASE ** (-jnp.arange(0, half, 2, dtype=F32) / half)
    ang_r = row_pos[:, None] * inv_freq
    ang_c = col_pos[:, None] * inv_freq
    ang = jnp.concatenate([ang_r, ang_r, ang_c, ang_c], axis=-1)
    cos, sin = jnp.cos(ang), jnp.sin(ang)
    first = (jnp.arange(dim) % (dim // 2)) < (dim // 4)
    sin_lo = jnp.where(first, -sin, 0.0)
    sin_hi = jnp.where(first, 0.0, sin)
    return cos, sin_lo, sin_hi


def _tile_lanes(tabs, reps):
    return tuple(jnp.tile(t, (1, reps)) for t in tabs)


def _pad_lanes(tabs, width):
    cos, lo, hi = tabs
    pad = width - cos.shape[1]
    return (jnp.pad(cos, ((0, 0), (0, pad)), constant_values=1.0), jnp.pad(lo, ((0, 0), (0, pad))),
            jnp.pad(hi, ((0, 0), (0, pad))))


def _permute_w_in(w_in):
    offs = {}
    o = 0
    for name, size in zip(SPLIT_NAMES, SPLIT_SIZES):
        offs[name] = (o, size)
        o += size
    cols = []
    for names, width in zip(OUT_GROUPS, OUT_WIDTHS):
        used = 0
        for n in names:
            s, size = offs[n]
            cols.append(w_in[:, :, s:s + size])
            used += size
        if used < width:
            cols.append(jnp.zeros(w_in.shape[:2] + (width - used,), w_in.dtype))
    return jnp.concatenate(cols, axis=-1).astype(BF16)


def _param_row(vals, col0, width=128, rows=8):
    out = jnp.zeros((DEPTH, rows, width), F32)
    for r, v in enumerate(vals):
        out = out.at[:, r, col0:col0 + v.shape[1]].set(v.astype(F32))
    return out


def kernel(x_prompt, x_sample, cache_diff_k, cache_diff_v, state_delta, cache_mla_ckv, cache_mla_krope, state_ssm, c, c_ctx, w_mod, b_mod, w_in, diff_lam, diff_norm, dn_conv, dn_a_log, dn_dt_bias, dn_norm, mla_q_norm, mla_kv_norm, mla_w_uq, mla_w_uk, mla_w_uv, ssm_conv_w, ssm_conv_b, ssm_a_log, ssm_dt_bias, ssm_d, ssm_norm, w_out, ln1_g, ln1_b, ln2_g, ln2_b, w_ff1, w_ff2):
    nb_p, len_p = x_prompt.shape[:2]
    nb_s, len_s = x_sample.shape[:2]
    past_len = cache_diff_k.shape[2]
    assert past_len == TILE and nb_s + 1 <= 8

    cvec = jnp.concatenate([c_ctx[None, :], c, jnp.zeros((8 - 1 - nb_s, D_MODEL), F32)], axis=0)
    mods = _mod_call(cvec, w_mod, b_mod).reshape(DEPTH, 8, 6, D_MODEL)

    w_in_p = _permute_w_in(w_in)
    w_uq = mla_w_uq.reshape(DEPTH, Q_RANK, H_C, NOPE_C + ROPE_C)
    w_uq_p = jnp.concatenate([w_uq[..., :NOPE_C].reshape(DEPTH, Q_RANK, H_C * NOPE_C),
                              w_uq[..., NOPE_C:].reshape(DEPTH, Q_RANK, H_C * ROPE_C)], axis=-1).astype(BF16)
    w_uk16, w_uv16 = mla_w_uk.astype(BF16), mla_w_uv.astype(BF16)
    w_out16, w_ff1_16, w_ff2_16 = w_out.astype(BF16), w_ff1.astype(BF16), w_ff2.astype(BF16)
    diff_g = jnp.tile(diff_norm, (1, H_A))[:, None, :]
    lam_init = jnp.asarray([0.8 - 0.6 * math.exp(-0.3 * l) for l in range(DEPTH)], F32)
    lam_rows = jnp.concatenate([diff_lam, jnp.broadcast_to(lam_init[:, None, None], (DEPTH, 4, DQK_A))], axis=1)
    dn_g = jnp.tile(dn_norm, (1, H_B))[:, None, :]
    ss_g = ssm_norm[:, None, :]
    dn_prm = _param_row([dn_a_log.reshape(DEPTH, 2 * H_B), dn_dt_bias.reshape(DEPTH, 2 * H_B)], COL_DECAY)
    ssm_prm = _param_row([ssm_a_log.reshape(DEPTH, 2 * H_D), ssm_dt_bias.reshape(DEPTH, 2 * H_D)], COL_DT, width=256)
    ssm_prm = ssm_prm.at[:, 2, :].set(jnp.repeat(ssm_d, P_D, axis=1))

    tabs_a = _tile_lanes(_rope_tables(len_s, DQK_A, 256), 256 // DQK_A)
    tabs_c = _rope_tables(len_s, ROPE_C, 128)
    tabs_small = _pad_lanes(tabs_c, 128)
    tabs_q = _tile_lanes(tabs_c, 128 // ROPE_C)

    past_kr = jnp.pad(cache_mla_krope, ((0, 0), (0, 0), (0, 0), (0, 128 - ROPE_C)))

    def layer(x, l, batch, seq_len, is_latent):
        mod = mods[l]
        (a_qkv, b_qkv, b_gate, c_q, d_z, d_xbc, c_kv, small) = _inproj(
            x, mod, w_in_p[l], seq_len, is_latent, tabs_a + tabs_small if is_latent else None)
        if is_latent:
            past_a = (cache_diff_k[:, l].reshape(batch, past_len, 256), cache_diff_v[:, l].reshape(batch, past_len, 256))
            past_c = (cache_mla_ckv[:, l], past_kr[:, l])
            s0_b, s0_d = state_delta[:, l], state_ssm[:, l]
        else:
            past_a = past_c = s0_b = s0_d = None
        o_a = _diff_attn(a_qkv, batch, seq_len, lam_rows[l], diff_g[l], past_a)
        o_c, ckv_n = _mla_attn(c_q, c_kv, small, batch, seq_len,
                               (mla_q_norm[l][None, :], mla_kv_norm[l][None, :], w_uq_p[l], w_uk16[l], w_uv16[l]),
                               past_c, tabs_q if is_latent else None)
        o_bf, o_bb, st_b = _deltanet(b_qkv, small, batch, seq_len, dn_conv[l], dn_prm[l], s0_b)
        y_d2, st_d = _ssd(d_xbc, small, batch, seq_len, ssm_conv_w[l], ssm_conv_b[l][None, :], ssm_prm[l], s0_d)
        x = _outproj_ffn(x, mod, o_a, o_bf, o_bb, b_gate, o_c, y_d2, d_z,
                         (dn_g[l], ss_g[l], w_out16[l], ln1_g[l][None, :], ln1_b[l][None, :], w_ff1_16[l], w_ff2_16[l],
                          ln2_g[l][None, :], ln2_b[l][None, :]), seq_len, is_latent)
        return x, (a_qkv, st_b, ckv_n, small, st_d)

    xp = x_prompt.reshape(nb_p * len_p, D_MODEL)
    xs = x_sample.reshape(nb_s * len_s, D_MODEL)
    ctx = ([], [], [], [], [], [])
    for l in range(DEPTH):
        xp, (a_qkv, st_b, ckv_n, small, st_d) = layer(xp, l, nb_p, len_p, False)
        xs, _ = layer(xs, l, nb_s, len_s, True)
        ctx[0].append(a_qkv[:, 256:512].reshape(nb_p, len_p, H_A, 2, DQK_A))
        ctx[1].append(a_qkv[:, 512:768].reshape(nb_p, len_p, H_A, DV_A))
        ctx[2].append(st_b)
        ctx[3].append(ckv_n.reshape(nb_p, len_p, KV_RANK))
        ctx[4].append(small[:, COL_KR:COL_KR + ROPE_C].reshape(nb_p, len_p, ROPE_C))
        ctx[5].append(st_d)
    return (xp.reshape(nb_p, len_p, D_MODEL), xs.reshape(nb_s, len_s, D_MODEL)) + tuple(
        jnp.stack(t, axis=1) for t in ctx)
```

```python
import functools
import math

import jax
import jax.numpy as jnp
import numpy as np
from jax import lax
from jax.experimental import pallas as pl
from jax.experimental.pallas import tpu as pltpu

F32 = jnp.float32
BF16 = jnp.bfloat16

D_MODEL = 1024
DEPTH = 4
GRID_W = 64
GROUP_W = 256
H_A, DV_A, DQK_A = 4, 64, 32
H_B, DK_B, DV_B = 4, 64, 64
H_C, NOPE_C, ROPE_C, V_C = 4, 64, 32, 64
Q_RANK, KV_RANK = 256, 128
H_D, P_D, N_D, G_D = 4, 64, 128, 2
CHUNK = 64
D_FF = 4 * D_MODEL
ROPE_BASE = 10000.0
EPS = 1e-6
ALPHA = (2 * DEPTH) ** 0.25
LOG2E = 1.4426950408889634

SPLIT_NAMES = ("a_q", "a_k", "a_v", "b_qkv", "b_beta", "b_decay", "b_gate",
               "c_q", "c_kv", "c_kr", "d_z", "d_xbc", "d_dt")
SPLIT_SIZES = (256, 256, 256, 768, 8, 8, 256, 256, 128, 32, 256, 768, 8)
OUT_GROUPS = (("a_q", "a_k", "a_v"), ("b_qkv",), ("b_gate",), ("c_q",), ("d_z",), ("d_xbc",),
              ("c_kv",), ("c_kr", "b_beta", "b_decay", "d_dt"))
OUT_WIDTHS = (768, 768, 256, 256, 256, 768, 128, 128)
IN_COLS_PAD = sum(OUT_WIDTHS)
COL_KR, COL_BETA, COL_DECAY, COL_DT = 0, 32, 40, 48

TILE = 256
TQ = 256
TB = 512
VMEM_LIMIT = 56 * 1024 * 1024


def _cparams(n_axes):
    return pltpu.CompilerParams(dimension_semantics=("arbitrary",) * n_axes,
                                vmem_limit_bytes=VMEM_LIMIT)


def _layer_spec(stacked, layer, **kwargs):
    zeros = (0,) * (stacked.ndim - 1)
    return pl.BlockSpec((None,) + stacked.shape[1:], lambda *_: (layer,) + zeros, **kwargs)


def _dot(a, b):
    return jnp.dot(a, b, preferred_element_type=F32)


def _dot_nt(a, b):
    return lax.dot_general(a, b, (((1,), (1,)), ((), ())), preferred_element_type=F32)


def _split2(x):
    hi = x.astype(BF16)
    return hi, (x - hi.astype(F32)).astype(BF16)


def _sigmoid(x):
    return 1.0 / (1.0 + jnp.exp(-x))


def _silu(x):
    return x * _sigmoid(x)


def _softplus(x):
    return jnp.maximum(x, 0.0) + jnp.log1p(jnp.exp(-jnp.abs(x)))


def _rms_groups(x, width):
    outs = []
    for s in range(0, x.shape[1], width):
        xg = x[:, s:s + width]
        ms = jnp.mean(xg * xg, axis=-1, keepdims=True)
        outs.append(xg * lax.rsqrt(ms + EPS))
    return outs[0] if len(outs) == 1 else jnp.concatenate(outs, axis=1)


def _layer_norm(x, g, b):
    mu = jnp.mean(x, axis=-1, keepdims=True)
    xc = x - mu
    var = jnp.mean(xc * xc, axis=-1, keepdims=True)
    return xc * lax.rsqrt(var + EPS) * g + b


def _rope(x, cos, sin_lo, sin_hi):
    w = x.shape[1]
    return x * cos + pltpu.roll(x, w - 8, 1) * sin_lo + pltpu.roll(x, 8, 1) * sin_hi


def _mod_kernel(c_ref, w_ref, b_ref, o_ref):
    c = c_ref[...]
    o_ref[...] = _dot(_silu(c).astype(BF16), w_ref[...].astype(BF16)) + b_ref[...]


def _mod_call(cvec, w_mod, b_mod):
    nblk = 4
    wcol = 6 * D_MODEL // nblk
    return pl.pallas_call(
        _mod_kernel,
        grid=(DEPTH, nblk),
        in_specs=[pl.BlockSpec((8, D_MODEL), lambda l, j: (0, 0)),
                  pl.BlockSpec((None, D_MODEL, wcol), lambda l, j: (l, 0, j)),
                  pl.BlockSpec((None, 1, wcol), lambda l, j: (l, 0, j))],
        out_specs=pl.BlockSpec((None, 8, wcol), lambda l, j: (l, 0, j)),
        out_shape=jax.ShapeDtypeStruct((DEPTH, 8, 6 * D_MODEL), F32),
        compiler_params=_cparams(2),
        name="mod",
    )(cvec, w_mod, b_mod.reshape(DEPTH, 1, 6 * D_MODEL))


def _inproj_kernel(has_rope, *refs):
    if has_rope:
        (x_ref, mod_ref, w_ref, cos_ref, slo_ref, shi_ref, cos_s_ref, slo_s_ref, shi_s_ref,
         oa, obq, obg, ocq, odz, odx, ockv, osm) = refs
    else:
        x_ref, mod_ref, w_ref, oa, obq, obg, ocq, odz, odx, ockv, osm = refs
    x = x_ref[...]
    h = (x * (1.0 + mod_ref[1:2, :]) + mod_ref[0:1, :]).astype(BF16)
    outs = (oa, obq, obg, ocq, odz, odx, ockv, osm)
    off = 0
    for idx, (o_ref, wd) in enumerate(zip(outs, OUT_WIDTHS)):
        y = _dot(h, w_ref[:, off:off + wd])
        off += wd
        if has_rope and idx == 0:
            cos, slo, shi = cos_ref[...], slo_ref[...], shi_ref[...]
            o_ref[:, 0:256] = _rope(y[:, 0:256], cos, slo, shi)
            o_ref[:, 256:512] = _rope(y[:, 256:512], cos, slo, shi)
            o_ref[:, 512:768] = y[:, 512:768]
        elif has_rope and idx == 7:
            o_ref[...] = _rope(y, cos_s_ref[...], slo_s_ref[...], shi_s_ref[...])
        else:
            o_ref[...] = y


def _inproj(x, mods, w_in_p, layer, seq_len, per_batch_mod, rope_tabs):
    t = x.shape[0]
    nb_seq = seq_len // TB
    has_rope = rope_tabs is not None
    mod_idx = (lambda i: (layer, 1 + i // nb_seq, 0, 0)) if per_batch_mod else (lambda i: (layer, 0, 0, 0))
    in_specs = [pl.BlockSpec((TB, D_MODEL), lambda i: (i, 0)),
                pl.BlockSpec((None, None, 6, D_MODEL), mod_idx),
                _layer_spec(w_in_p, layer)]
    args = [x, mods, w_in_p]
    if has_rope:
        for tab in rope_tabs:
            in_specs.append(pl.BlockSpec((TB, tab.shape[1]), lambda i: (i % nb_seq, 0)))
            args.append(tab)
    return pl.pallas_call(
        functools.partial(_inproj_kernel, has_rope),
        grid=(t // TB,),
        in_specs=in_specs,
        out_specs=[pl.BlockSpec((TB, wd), lambda i: (i, 0)) for wd in OUT_WIDTHS],
        out_shape=[jax.ShapeDtypeStruct((t, wd), F32) for wd in OUT_WIDTHS],
        compiler_params=_cparams(1),
        name="inproj",
    )(*args)


VT_ROWS = 80


def _key_halves(k_ref):
    half = k_ref.shape[0] // 2
    return [slice(0, half), slice(half, k_ref.shape[0])]


def _scores_t(k_ref, col0, qt16):
    return [_dot(k_ref[rows, col0:col0 + 128], qt16) for rows in _key_halves(k_ref)]


def _softmax_pv_t(s, k_ref, vt_ref, h):
    tile_max = [jnp.max(x.reshape(x.shape[0] // 128, 128, x.shape[1]), axis=0) for x in s]
    m = jnp.max(jnp.maximum(tile_max[0], tile_max[1]), axis=0, keepdims=True)
    acc = None
    for rows, x in zip(_key_halves(k_ref), s):
        part = _dot(vt_ref[h, :, rows], jnp.exp2(x - m).astype(BF16))
        acc = part if acc is None else acc + part
    return acc[0:64] / acc[64:65]


def _attend_all_t(k_ref, vt_ref, jobs):
    outs = []
    s_next = _scores_t(k_ref, jobs[0][0], jobs[0][1])
    for i, (_, _, h) in enumerate(jobs):
        s = s_next
        if i + 1 < len(jobs):
            s_next = _scores_t(k_ref, jobs[i + 1][0], jobs[i + 1][1])
        outs.append(_softmax_pv_t(s, k_ref, vt_ref, h))
    return outs


def _diff_attn_kernel(has_past, nch_own, tq, *refs):
    if has_past:
        q_ref, k_ref, v_ref, pk_ref, pv_ref, lam_ref, g_ref, o_ref, k_scr, vt_scr = refs
    else:
        q_ref, k_ref, v_ref, lam_ref, g_ref, o_ref, k_scr, vt_scr = refs

    @pl.when(pl.program_id(1) == 0)
    def _prep():
        def put(c, k, v):
            rows = slice(c * TILE, (c + 1) * TILE)
            k_scr[rows, :] = k.astype(BF16)
            vt = v.T
            for h in range(H_A):
                vt_scr[h, 0:DV_A, rows] = vt[h * DV_A:(h + 1) * DV_A, :].astype(BF16)

        vt_scr[:, DV_A:VT_ROWS, :] = jnp.ones((H_A, VT_ROWS - DV_A, vt_scr.shape[2]), BF16)
        for c in range(nch_own):
            put(c, k_ref[c * TILE:(c + 1) * TILE, :], v_ref[c * TILE:(c + 1) * TILE, :])
        if has_past:
            put(nch_own, pk_ref[...], pv_ref[...])

    lp = lam_ref[...]
    lam_init = lam_ref[4:5, 0:1]
    lam = (jnp.exp(jnp.sum(lp[0:1] * lp[1:2], axis=-1, keepdims=True))
           - jnp.exp(jnp.sum(lp[2:3] * lp[3:4], axis=-1, keepdims=True)) + lam_init)
    qt = (q_ref[...] * (DQK_A ** -0.5 * LOG2E)).T
    slot = lax.broadcasted_iota(jnp.int32, (128, tq), 0) // DQK_A
    jobs = []
    for h in range(H_A):
        grp = (h * 2 * DQK_A) // 128
        qg = qt[grp * 128:(grp + 1) * 128, :]
        for t in range(2):
            qm = jnp.where(slot == (h * 2 + t) % (128 // DQK_A), qg, 0.0).astype(BF16)
            jobs.append((grp * 128, qm, h))
    outs = _attend_all_t(k_scr, vt_scr, jobs)
    heads = [outs[2 * h] - lam * outs[2 * h + 1] for h in range(H_A)]
    o = jnp.concatenate(heads, axis=0).T
    o_ref[...] = _rms_groups(o, DV_A) * g_ref[...] * (1.0 - lam_init)


def _diff_attn(a_qkv, batch, seq_len, lam_p, g_row, past):
    tq = min(TQ, seq_len)
    nq = seq_len // tq
    nch_own = seq_len // TILE
    has_past = past is not None
    n_keys = seq_len + (TILE if has_past else 0)
    in_specs = [pl.BlockSpec((tq, 256), lambda b, i: (b * nq + i, 0)),
                pl.BlockSpec((seq_len, 256), lambda b, i: (b, 1)),
                pl.BlockSpec((seq_len, 256), lambda b, i: (b, 2))]
    args = [a_qkv, a_qkv, a_qkv]
    if has_past:
        in_specs += [pl.BlockSpec((None, TILE, 256), lambda b, i: (b, 0, 0))] * 2
        args += list(past)
    in_specs += [pl.BlockSpec((8, DQK_A), lambda b, i: (0, 0)), pl.BlockSpec((1, 256), lambda b, i: (0, 0))]
    args += [lam_p, g_row]
    return pl.pallas_call(
        functools.partial(_diff_attn_kernel, has_past, nch_own, tq),
        grid=(batch, nq),
        in_specs=in_specs,
        out_specs=pl.BlockSpec((tq, 256), lambda b, i: (b * nq + i, 0)),
        out_shape=jax.ShapeDtypeStruct((batch * seq_len, 256), F32),
        scratch_shapes=[pltpu.VMEM((n_keys, 256), BF16), pltpu.VMEM((H_A, VT_ROWS, n_keys), BF16)],
        compiler_params=_cparams(2),
        name="diff_attn",
    )(*args)


def _mla_kernel(has_past, has_rope, nch_own, tq, *refs):
    refs = list(refs)
    cq_ref, ckv_ref, sm_ref = refs[:3]
    pos = 3
    if has_past:
        pckv_ref, pkr_ref = refs[pos:pos + 2]
        pos += 2
    qn_ref, kvn_ref, wuq_ref, wuk_ref, wuv_ref = refs[pos:pos + 5]
    pos += 5
    if has_rope:
        cos_ref, slo_ref, shi_ref = refs[pos:pos + 3]
        pos += 3
    o_ref, ckv_out = refs[pos:pos + 2]
    k_scr, vt_scr = refs[pos + 2:]

    def put(c, ckv, kr):
        rows = slice(c * TILE, (c + 1) * TILE)
        c16 = ckv.astype(BF16)
        kn = _dot(c16, wuk_ref[...])
        vt = _dot(c16, wuv_ref[...]).T
        lane = lax.broadcasted_iota(jnp.int32, kr.shape, 1)
        kr16 = jnp.where(lane < ROPE_C, kr, 0.0)[:, 0:128 - NOPE_C].astype(BF16)
        for h in range(H_C):
            k_scr[rows, h * 128:h * 128 + NOPE_C] = kn[:, h * NOPE_C:(h + 1) * NOPE_C].astype(BF16)
            k_scr[rows, h * 128 + NOPE_C:(h + 1) * 128] = kr16
            vt_scr[h, 0:V_C, rows] = vt[h * V_C:(h + 1) * V_C, :].astype(BF16)

    @pl.when(pl.program_id(1) == 0)
    def _prep():
        vt_scr[:, V_C:VT_ROWS, :] = jnp.ones((H_C, VT_ROWS - V_C, vt_scr.shape[2]), BF16)
        for c in range(nch_own):
            rows = slice(c * TILE, (c + 1) * TILE)
            ckv = _rms_groups(ckv_ref[rows, :], KV_RANK) * kvn_ref[...]
            ckv_out[rows, :] = ckv
            put(c, ckv, sm_ref[rows, :])
        if has_past:
            put(nch_own, pckv_ref[...], pkr_ref[...])

    cqn = (_rms_groups(cq_ref[...], Q_RANK) * qn_ref[...]).astype(BF16)
    qa = _dot(cqn, wuq_ref[...])
    q_nope = qa[:, 0:256]
    q_rope = qa[:, 256:384]
    if has_rope:
        q_rope = _rope(q_rope, cos_ref[...], slo_ref[...], shi_ref[...])
    scale = (NOPE_C + ROPE_C) ** -0.5 * LOG2E
    zpad = jnp.zeros((tq, 128 - NOPE_C - ROPE_C), F32)
    jobs = []
    for h in range(H_C):
        qh = jnp.concatenate([q_nope[:, h * NOPE_C:(h + 1) * NOPE_C],
                              q_rope[:, h * ROPE_C:(h + 1) * ROPE_C], zpad], axis=1) * scale
        jobs.append((h * 128, qh.T.astype(BF16), h))
    o_ref[...] = jnp.concatenate(_attend_all_t(k_scr, vt_scr, jobs), axis=0).T


def _mla_attn(c_q, c_kv, small, batch, seq_len, norms, stacked_wts, layer, past, rope_tabs):
    tq = min(TQ, seq_len)
    nq = seq_len // tq
    nch_own = seq_len // TILE
    has_past = past is not None
    has_rope = rope_tabs is not None
    n_keys = seq_len + (TILE if has_past else 0)
    const = lambda b, i: (0, 0)
    in_specs = [pl.BlockSpec((tq, Q_RANK), lambda b, i: (b * nq + i, 0)),
                pl.BlockSpec((seq_len, KV_RANK), lambda b, i: (b, 0)),
                pl.BlockSpec((seq_len, 128), lambda b, i: (b, 0))]
    args = [c_q, c_kv, small]
    if has_past:
        in_specs += [pl.BlockSpec((None, TILE, 128), lambda b, i: (b, 0, 0))] * 2
        args += list(past)
    in_specs += [pl.BlockSpec((1, Q_RANK), const), pl.BlockSpec((1, KV_RANK), const)]
    in_specs += [_layer_spec(w, layer) for w in stacked_wts]
    args += list(norms) + list(stacked_wts)
    if has_rope:
        in_specs += [pl.BlockSpec((tq, 128), lambda b, i: (i, 0))] * 3
        args += list(rope_tabs)
    return pl.pallas_call(
        functools.partial(_mla_kernel, has_past, has_rope, nch_own, tq),
        grid=(batch, nq),
        in_specs=in_specs,
        out_specs=[pl.BlockSpec((tq, 256), lambda b, i: (b * nq + i, 0)),
                   pl.BlockSpec((seq_len, KV_RANK), lambda b, i: (b, 0))],
        out_shape=[jax.ShapeDtypeStruct((batch * seq_len, 256), F32),
                   jax.ShapeDtypeStruct((batch * seq_len, KV_RANK), F32)],
        scratch_shapes=[pltpu.VMEM((n_keys, 512), BF16), pltpu.VMEM((H_C, VT_ROWS, n_keys), BF16)],
        compiler_params=_cparams(2),
        name="mla_attn",
    )(*args)


def _conv3(main, prev8, next8, w, has_prev, has_next):
    n = main.shape[0]
    row = lax.broadcasted_iota(jnp.int32, main.shape, 0)
    before = jnp.where(has_prev, prev8[7:8, :], 0.0)
    after = jnp.where(has_next, next8[0:1, :], 0.0)
    xm = jnp.where(row == 0, before, pltpu.roll(main, 1, 0))
    xp = jnp.where(row == n - 1, after, pltpu.roll(main, n - 1, 0))
    return xm * w[0:1, :] + main * w[1:2, :] + xp * w[2:3, :]


def _tile_specs(batch, seq_len, widths, reverse):
    nt = seq_len // TILE
    last8 = batch * seq_len // 8 - 1

    def tile(b, i):
        return b * nt + ((nt - 1 - i) if reverse else i)

    specs = []
    for width, halos in widths:
        specs.append(pl.BlockSpec((TILE, width), lambda b, i: (tile(b, i), 0)))
        if halos:
            specs.append(pl.BlockSpec((8, width), lambda b, i: (jnp.maximum(tile(b, i) * (TILE // 8) - 1, 0), 0)))
            specs.append(pl.BlockSpec((8, width), lambda b, i: (jnp.minimum((tile(b, i) + 1) * (TILE // 8), last8), 0)))
    return specs


def _deltanet_kernel(nt, has_state, *refs):
    refs = list(refs)
    tiles = (refs[0:4], refs[4:8])
    w_ref, prm_ref, cf_ref, cb_ref, ce_ref = refs[8:13]
    n_in = 14 if has_state else 13
    of_ref, ob_ref, sfin_ref, s_scr = refs[n_in:]
    step = pl.program_id(1)
    n_chunks = TILE // CHUNK

    @pl.when(step == 0)
    def _init():
        s_scr[...] = jnp.zeros(s_scr.shape, F32)
        if has_state:
            for d in range(2):
                for h in range(H_B):
                    s_scr[d, h * DK_B:(h + 1) * DK_B, h * DV_B:(h + 1) * DV_B] = refs[13][d, h]

    block16 = cb_ref[CB_SAME]
    block = cf_ref[CF_SAME]

    def expand(x, e16):
        hi, lo = _split2(x)
        return _dot(hi, e16) + _dot(lo, e16)

    def head_sums(x):
        hi, lo = _split2(x)
        return _dot(hi, block16) + _dot(lo, block16)

    chains, dirs = [], []
    shared = None
    for dirn in (0, 1):
        main_ref, prev_ref, next_ref, sm_ref = tiles[dirn]
        tpos = (nt - 1 - step) if dirn == 1 else step
        if shared is None or nt > 1:
            qkv = _silu(_conv3(main_ref[...], prev_ref[...], next_ref[...], w_ref[...], tpos > 0, tpos < nt - 1))
            q, k, v = qkv[:, 0:256], qkv[:, 256:512], qkv[:, 512:768]
            qn = q * lax.rsqrt(head_sums(q * q) + EPS) * (DK_B ** -0.5)
            kn = k * lax.rsqrt(head_sums(k * k) + EPS)
            knt16 = kn.T.astype(BF16)
            sm = sm_ref[...]
            lane = lax.broadcasted_iota(jnp.int32, sm.shape, 1)
            beta = _sigmoid(sm)
            g = -jnp.exp(prm_ref[0:1, :]) * _softplus(sm + prm_ref[1:2, :])
            g = jnp.where(jnp.logical_and(lane >= COL_DECAY, lane < COL_DECAY + 2 * H_B), g, 0.0)
            shared = (qn, kn, knt16, v, beta, g)
        qn, kn, knt16, v, beta, g = shared
        beta_x = expand(beta, ce_ref[CE_BETA + dirn])
        g_hi, g_lo = _split2(expand(g, ce_ref[CE_DECAY + dirn]))
        gc = _dot(cb_ref[dirn], g_hi) + _dot(cb_ref[dirn], g_lo)
        gt = _dot(block16, g_hi) + _dot(block16, g_lo)
        gc_t = gc.T
        egc = jnp.exp(gc)
        kb = kn * beta_x
        dirs.append(dict(vb=v * beta_x, kbe=kb * egc, qd=qn * egc, egl=jnp.exp(gt),
                         kdt=(kn * jnp.exp(gt - gc)).T.astype(BF16), s=s_scr[dirn], o=[None] * n_chunks))
        kb16, qn16 = kb.astype(BF16), qn.astype(BF16)
        for h in range(H_B):
            lanes16 = cb_ref[CB_LANES + h]
            dec = jnp.exp((gc[:, h * DK_B:h * DK_B + 1] - gc_t[h * DK_B:h * DK_B + 1, :]) + cf_ref[dirn])
            a = _dot(kb16 * lanes16, knt16) * dec * cf_ref[CF_OFFDIAG]
            chains.append(dict(dirn=dirn, h=h, a=a, r=-(a * cf_ref[CF_LEVEL]),
                               qk16=(_dot(qn16 * lanes16, knt16) * dec).astype(BF16)))

    for lvl in range(1, CHUNK.bit_length() - 1):
        for ch in chains:
            e = ch["a"] * cf_ref[CF_LEVEL + lvl]
            x = e + _dot(ch["r"].astype(BF16), e.astype(BF16))
            ch["r"] = ch["r"] - x - _dot(x.astype(BF16), ch["r"].astype(BF16))

    for dirn, dd in enumerate(dirs):
        mine = [ch for ch in chains if ch["dirn"] == dirn]
        vb16, kbe16 = dd["vb"].astype(BF16), dd["kbe"].astype(BF16)
        u, w = dd["vb"], dd["kbe"]
        for ch in mine:
            r16, lanes16 = ch["r"].astype(BF16), cb_ref[CB_LANES + ch["h"]]
            u = u + _dot(r16, vb16 * lanes16)
            w = w + _dot(r16, kbe16 * lanes16)
        u16, w16 = u.astype(BF16), w.astype(BF16)
        qku, qkw = None, None
        for ch in mine:
            lanes16 = cb_ref[CB_LANES + ch["h"]]
            a_u, a_w = _dot(ch["qk16"], u16 * lanes16), _dot(ch["qk16"], w16 * lanes16)
            qku = a_u if qku is None else qku + a_u
            qkw = a_w if qkw is None else qkw + a_w
        dd["qku"] = qku
        dd["qp16"] = (dd["qd"] - qkw).astype(BF16)
        dd["wu16"] = jnp.concatenate([w16, u16], axis=1)

    for ci in range(n_chunks):
        for dirn, dd in enumerate(dirs):
            c = (n_chunks - 1 - ci) if dirn == 1 else ci
            rows = slice(c * CHUNK, (c + 1) * CHUNK)
            s16 = dd["s"].astype(BF16)
            dd["o"][c] = _dot(dd["qp16"][rows], s16) + dd["qku"][rows]
            pb = _dot(dd["kdt"] * cb_ref[CB_LANES + c], dd["wu16"])
            p16 = (pb[:, 0:256] * block).astype(BF16)
            dd["s"] = dd["s"] * dd["egl"][c * CHUNK:c * CHUNK + 1, :] - _dot(p16, s16) + pb[:, 256:512] * block

    for dirn, (dd, o_ref) in enumerate(zip(dirs, (of_ref, ob_ref))):
        s_scr[dirn] = dd["s"]
        o_ref[...] = jnp.concatenate(dd["o"], axis=0)

    @pl.when(step == nt - 1)
    def _fin():
        for d in range(2):
            for h in range(H_B):
                sfin_ref[d, h] = s_scr[d, h * DK_B:(h + 1) * DK_B, h * DV_B:(h + 1) * DV_B]


CF_NINF, CF_OFFDIAG, CF_LEVEL, CF_SAME = 0, 2, 3, 9
CB_RUN, CB_SAME, CB_LANES = 0, 2, 3
CE_BETA, CE_DECAY, CE_DT = 0, 2, 4


def _scan_consts():
    i = np.arange(TILE)[:, None]
    j = np.arange(TILE)[None, :]
    same = (i // CHUNK) == (j // CHUNK)
    run = [same & (i >= j), same & (i <= j)]
    levels = [((i >> (l + 1)) == (j >> (l + 1))) & ((i >> l) != (j >> l)) for l in range(CHUNK.bit_length() - 1)]
    cf = np.stack([np.where(m, 0.0, -np.inf) for m in run] + [i != j] + levels + [same]).astype(np.float32)
    cb = np.stack(run + [same] + [np.broadcast_to(j // CHUNK == c, (TILE, TILE)) for c in range(TILE // CHUNK)])
    rows = np.arange(128)[:, None]
    ce = np.stack([rows == col0 + d * H_B + j // DK_B for col0 in (COL_BETA, COL_DECAY, COL_DT) for d in range(2)])
    return jnp.asarray(cf), jnp.asarray(cb, BF16), jnp.asarray(ce, BF16)


def _deltanet(b_qkv, small, batch, seq_len, conv_w, prm, s0):
    nt = seq_len // TILE
    has_state = s0 is not None
    const = lambda b, i: (0, 0)
    const3 = lambda b, i: (0, 0, 0)
    consts = _scan_consts()
    widths = ((768, True), (128, False))
    in_specs = (_tile_specs(batch, seq_len, widths, False) + _tile_specs(batch, seq_len, widths, True)
                + [pl.BlockSpec((3, 768), const), pl.BlockSpec((8, 128), const)]
                + [pl.BlockSpec(cst.shape, const3) for cst in consts])
    args = [b_qkv, b_qkv, b_qkv, small] * 2 + [conv_w, prm] + list(consts)
    state_spec = pl.BlockSpec((None, 2, H_B, DK_B, DV_B), lambda b, i: (b, 0, 0, 0, 0))
    if has_state:
        in_specs.append(state_spec)
        args.append(s0)
    return pl.pallas_call(
        functools.partial(_deltanet_kernel, nt, has_state),
        grid=(batch, nt),
        in_specs=in_specs,
        out_specs=[pl.BlockSpec((TILE, 256), lambda b, i: (b * nt + i, 0)),
                   pl.BlockSpec((TILE, 256), lambda b, i: (b * nt + nt - 1 - i, 0)), state_spec],
        out_shape=[jax.ShapeDtypeStruct((batch * seq_len, 256), F32)] * 2
        + [jax.ShapeDtypeStruct((batch, 2, H_B, DK_B, DV_B), F32)],
        scratch_shapes=[pltpu.VMEM((2, H_B * DK_B, H_B * DV_B), F32)],
        compiler_params=_cparams(2),
        name="deltanet",
    )(*args)


def _ssd_kernel(nt, has_state, *refs):
    refs = list(refs)
    tiles = (refs[0:4], refs[4:8])
    w_ref, bias_ref, prm_ref, cf_ref, cb_ref, ce_ref = refs[8:14]
    n_in = 15 if has_state else 14
    yf_ref, yb_ref, sfin_ref, s_scr = refs[n_in:]
    step = pl.program_id(1)
    n_chunks = TILE // CHUNK
    glanes = (H_D // G_D) * P_D

    @pl.when(step == 0)
    def _init():
        if has_state:
            for d in range(2):
                s_scr[d] = jnp.concatenate([refs[14][d, h] for h in range(H_D)], axis=0).T
        else:
            s_scr[...] = jnp.zeros(s_scr.shape, F32)

    block16 = cb_ref[CB_SAME]

    def expand(x, e16):
        hi, lo = _split2(x)
        return _dot(hi, e16) + _dot(lo, e16)

    def group_lanes(per_group):
        return jnp.concatenate([m[:, g * glanes:(g + 1) * glanes] for g, m in enumerate(per_group)], axis=1)

    dirs = []
    shared = None
    for dirn in (0, 1):
        main_ref, prev_ref, next_ref, sm_ref = tiles[dirn]
        tpos = (nt - 1 - step) if dirn == 1 else step
        if shared is None or nt > 1:
            xbc = _silu(_conv3(main_ref[...], prev_ref[...], next_ref[...], w_ref[...], tpos > 0, tpos < nt - 1)
                        + bias_ref[...])
            x = xbc[:, 0:256]
            bm16, cm16 = xbc[:, 256:512].astype(BF16), xbc[:, 512:768].astype(BF16)
            cbs = [_dot_nt(cm16[:, g * N_D:(g + 1) * N_D], bm16[:, g * N_D:(g + 1) * N_D]) for g in range(G_D)]
            bmt16 = xbc[:, 256:512].T.astype(BF16)
            sm = sm_ref[...]
            lane = lax.broadcasted_iota(jnp.int32, sm.shape, 1)
            dt = _softplus(sm + prm_ref[1:2, 0:128])
            da = jnp.where(jnp.logical_and(lane >= COL_DT, lane < COL_DT + 2 * H_D),
                           dt * (-jnp.exp(prm_ref[0:1, 0:128])), 0.0)
            shared = (x, cm16, cbs, bmt16, dt, da)
        x, cm16, cbs, bmt16, dt, da = shared
        e16 = ce_ref[CE_DT + dirn]
        a_hi, a_lo = _split2(expand(da, e16))
        ac = _dot(cb_ref[dirn], a_hi) + _dot(cb_ref[dirn], a_lo)
        at = _dot(block16, a_hi) + _dot(block16, a_lo)
        ac_t = ac.T
        xdt = x * expand(dt, e16)
        xdt16 = xdt.astype(BF16)
        y = None
        for h in range(H_D):
            seg = jnp.exp((ac[:, h * P_D:h * P_D + 1] - ac_t[h * P_D:h * P_D + 1, :]) + cf_ref[dirn])
            part = _dot((cbs[h * P_D // glanes] * seg).astype(BF16), xdt16 * cb_ref[CB_LANES + h])
            y = part if y is None else y + part
        if dirn == 0:
            y = y + prm_ref[2:3, :] * x
        dirs.append(dict(y=y, eac=jnp.exp(ac), eat=jnp.exp(at), xdd16=(xdt * jnp.exp(at - ac)).astype(BF16),
                         cm16=cm16, bmt16=bmt16, s=s_scr[dirn], rows=[None] * n_chunks))

    for ci in range(n_chunks):
        for dirn, dd in enumerate(dirs):
            c = (n_chunks - 1 - ci) if dirn == 1 else ci
            rows = slice(c * CHUNK, (c + 1) * CHUNK)
            s16 = dd["s"].astype(BF16)
            inter = group_lanes([_dot(dd["cm16"][rows, g * N_D:(g + 1) * N_D], s16) for g in range(G_D)])
            dd["rows"][c] = dd["y"][rows] + inter * dd["eac"][rows]
            upd = group_lanes([_dot(dd["bmt16"][g * N_D:(g + 1) * N_D, :] * cb_ref[CB_LANES + c, 0:N_D, :],
                                    dd["xdd16"]) for g in range(G_D)])
            dd["s"] = dd["s"] * dd["eat"][c * CHUNK:c * CHUNK + 1, :] + upd

    for dirn, (dd, y_ref) in enumerate(zip(dirs, (yf_ref, yb_ref))):
        s_scr[dirn] = dd["s"]
        y_ref[...] = jnp.concatenate(dd["rows"], axis=0)

    @pl.when(step == nt - 1)
    def _fin():
        for d in range(2):
            st = s_scr[d].T
            for h in range(H_D):
                sfin_ref[d, h] = st[h * P_D:(h + 1) * P_D, :]


def _ssd(d_xbc, small, batch, seq_len, conv_w, conv_b, prm, s0):
    nt = seq_len // TILE
    has_state = s0 is not None
    const = lambda b, i: (0, 0)
    const3 = lambda b, i: (0, 0, 0)
    consts = _scan_consts()
    widths = ((768, True), (128, False))
    in_specs = (_tile_specs(batch, seq_len, widths, False) + _tile_specs(batch, seq_len, widths, True)
                + [pl.BlockSpec((3, 768), const), pl.BlockSpec((1, 768), const), pl.BlockSpec((8, 256), const)]
                + [pl.BlockSpec(cst.shape, const3) for cst in consts])
    args = [d_xbc, d_xbc, d_xbc, small] * 2 + [conv_w, conv_b, prm] + list(consts)
    state_spec = pl.BlockSpec((None, 2, H_D, P_D, N_D), lambda b, i: (b, 0, 0, 0, 0))
    if has_state:
        in_specs.append(state_spec)
        args.append(s0)
    return pl.pallas_call(
        functools.partial(_ssd_kernel, nt, has_state),
        grid=(batch, nt),
        in_specs=in_specs,
        out_specs=[pl.BlockSpec((TILE, 256), lambda b, i: (b * nt + i, 0)),
                   pl.BlockSpec((TILE, 256), lambda b, i: (b * nt + nt - 1 - i, 0)), state_spec],
        out_shape=[jax.ShapeDtypeStruct((batch * seq_len, 256), F32)] * 2
        + [jax.ShapeDtypeStruct((batch, 2, H_D, P_D, N_D), F32)],
        scratch_shapes=[pltpu.VMEM((2, N_D, H_D * P_D), F32)],
        compiler_params=_cparams(2),
        name="ssd",
    )(*args)


def _outproj_ffn_kernel(x_ref, mod_ref, oa_ref, obf_ref, obb_ref, bg_ref, oc_ref, ydf_ref, ydb_ref, dz_ref,
                        dng_ref, ssg_ref, wo_ref, l1g_ref, l1b_ref, w1_ref, w2_ref, l2g_ref, l2b_ref, o_ref):
    x = x_ref[...]
    o_b = _rms_groups(obf_ref[...] + obb_ref[...], DV_B) * dng_ref[...] * _silu(bg_ref[...])
    y_d = (ydf_ref[...] + ydb_ref[...]) * _silu(dz_ref[...])
    o_d = _rms_groups(y_d, GROUP_W // G_D) * ssg_ref[...]
    mixed = None
    for idx, part in enumerate((oa_ref[...], o_b, oc_ref[...], o_d)):
        y = _dot(part.astype(BF16), wo_ref[idx * GROUP_W:(idx + 1) * GROUP_W, :])
        mixed = y if mixed is None else mixed + y
    x1 = _layer_norm(ALPHA * x + mod_ref[2:3, :] * mixed, l1g_ref[...], l1b_ref[...])
    h16 = (x1 * (1.0 + mod_ref[4:5, :]) + mod_ref[3:4, :]).astype(BF16)
    ff = None
    fcol = 1024
    for j in range(D_FF // fcol):
        a = jnp.square(jnp.maximum(_dot(h16, w1_ref[:, j * fcol:(j + 1) * fcol]), 0.0)).astype(BF16)
        y = _dot(a, w2_ref[j * fcol:(j + 1) * fcol, :])
        ff = y if ff is None else ff + y
    o_ref[...] = _layer_norm(ALPHA * x1 + mod_ref[5:6, :] * ff, l2g_ref[...], l2b_ref[...])


def _outproj_ffn(x, mods, o_a, o_bf, o_bb, b_gate, o_c, y_df, y_db, d_z, wts, layer, seq_len, per_batch_mod):
    t = x.shape[0]
    tb = min(TB, seq_len)
    nb_seq = seq_len // tb
    mod_idx = (lambda i: (layer, 1 + i // nb_seq, 0, 0)) if per_batch_mod else (lambda i: (layer, 0, 0, 0))
    row = lambda wd: pl.BlockSpec((tb, wd), lambda i: (i, 0))
    const = lambda i: (0, 0)
    single = pl.Buffered(1)
    dn_g, ss_g, w_out16, l1g, l1b, w_ff1_16, w_ff2_16, l2g, l2b = wts
    in_specs = [row(D_MODEL), pl.BlockSpec((None, None, 6, D_MODEL), mod_idx),
                row(256), row(256), row(256), row(256), row(256), row(256), row(256), row(256),
                pl.BlockSpec((1, 256), const), pl.BlockSpec((1, 256), const),
                _layer_spec(w_out16, layer, pipeline_mode=single),
                pl.BlockSpec((1, D_MODEL), const), pl.BlockSpec((1, D_MODEL), const),
                _layer_spec(w_ff1_16, layer, pipeline_mode=single),
                _layer_spec(w_ff2_16, layer, pipeline_mode=single),
                pl.BlockSpec((1, D_MODEL), const), pl.BlockSpec((1, D_MODEL), const)]
    return pl.pallas_call(
        _outproj_ffn_kernel,
        grid=(t // tb,),
        in_specs=in_specs,
        out_specs=row(D_MODEL),
        out_shape=jax.ShapeDtypeStruct((t, D_MODEL), F32),
        compiler_params=_cparams(1),
        name="outproj_ffn",
    )(x, mods, o_a, o_bf, o_bb, b_gate, o_c, y_df, y_db, d_z, *wts)


def _rope_tables(length, dim, width):
    rows = length // GRID_W
    row_pos = jnp.repeat(jnp.arange(rows, dtype=F32), GRID_W)
    col_pos = jnp.tile(jnp.arange(GRID_W, dtype=F32), rows)
    half = dim // 2
    inv_freq = ROPE_BASE ** (-jnp.arange(0, half, 2, dtype=F32) / half)
    ang_r = row_pos[:, None] * inv_freq
    ang_c = col_pos[:, None] * inv_freq
    ang = jnp.concatenate([ang_r, ang_r, ang_c, ang_c], axis=-1)
    cos, sin = jnp.cos(ang), jnp.sin(ang)
    first = (jnp.arange(dim) % (dim // 2)) < (dim // 4)
    sin_lo = jnp.where(first, -sin, 0.0)
    sin_hi = jnp.where(first, 0.0, sin)
    return cos, sin_lo, sin_hi


def _tile_lanes(tabs, reps):
    return tuple(jnp.tile(t, (1, reps)) for t in tabs)


def _pad_lanes(tabs, width):
    cos, lo, hi = tabs
    pad = width - cos.shape[1]
    return (jnp.pad(cos, ((0, 0), (0, pad)), constant_values=1.0), jnp.pad(lo, ((0, 0), (0, pad))),
            jnp.pad(hi, ((0, 0), (0, pad))))


def _permute_w_in(w_in):
    offs = {}
    o = 0
    for name, size in zip(SPLIT_NAMES, SPLIT_SIZES):
        offs[name] = (o, size)
        o += size
    cols = []
    for names, width in zip(OUT_GROUPS, OUT_WIDTHS):
        used = 0
        for n in names:
            s, size = offs[n]
            cols.append(w_in[:, :, s:s + size])
            used += size
        if used < width:
            cols.append(jnp.zeros(w_in.shape[:2] + (width - used,), w_in.dtype))
    return jnp.concatenate(cols, axis=-1).astype(BF16)


def _param_row(vals, col0, width=128, rows=8):
    out = jnp.zeros((DEPTH, rows, width), F32)
    for r, v in enumerate(vals):
        out = out.at[:, r, col0:col0 + v.shape[1]].set(v.astype(F32))
    return out


def kernel(x_prompt, x_sample, cache_diff_k, cache_diff_v, state_delta, cache_mla_ckv, cache_mla_krope, state_ssm, c, c_ctx, w_mod, b_mod, w_in, diff_lam, diff_norm, dn_conv, dn_a_log, dn_dt_bias, dn_norm, mla_q_norm, mla_kv_norm, mla_w_uq, mla_w_uk, mla_w_uv, ssm_conv_w, ssm_conv_b, ssm_a_log, ssm_dt_bias, ssm_d, ssm_norm, w_out, ln1_g, ln1_b, ln2_g, ln2_b, w_ff1, w_ff2):
    nb_p, len_p = x_prompt.shape[:2]
    nb_s, len_s = x_sample.shape[:2]
    past_len = cache_diff_k.shape[2]
    assert past_len == TILE and nb_s + 1 <= 8

    cvec = jnp.concatenate([c_ctx[None, :], c, jnp.zeros((8 - 1 - nb_s, D_MODEL), F32)], axis=0)
    mods = _mod_call(cvec, w_mod, b_mod).reshape(DEPTH, 8, 6, D_MODEL)

    w_in_p = _permute_w_in(w_in)
    w_uq = mla_w_uq.reshape(DEPTH, Q_RANK, H_C, NOPE_C + ROPE_C)
    w_uq_p = jnp.concatenate([w_uq[..., :NOPE_C].reshape(DEPTH, Q_RANK, H_C * NOPE_C),
                              w_uq[..., NOPE_C:].reshape(DEPTH, Q_RANK, H_C * ROPE_C)], axis=-1).astype(BF16)
    w_uk16, w_uv16 = mla_w_uk.astype(BF16), mla_w_uv.astype(BF16)
    w_out16, w_ff1_16, w_ff2_16 = w_out.astype(BF16), w_ff1.astype(BF16), w_ff2.astype(BF16)
    diff_g = jnp.tile(diff_norm, (1, H_A))[:, None, :]
    lam_init = jnp.asarray([0.8 - 0.6 * math.exp(-0.3 * l) for l in range(DEPTH)], F32)
    lam_rows = jnp.concatenate([diff_lam, jnp.broadcast_to(lam_init[:, None, None], (DEPTH, 4, DQK_A))], axis=1)
    dn_g = jnp.tile(dn_norm, (1, H_B))[:, None, :]
    ss_g = ssm_norm[:, None, :]
    dn_prm = _param_row([dn_a_log.reshape(DEPTH, 2 * H_B), dn_dt_bias.reshape(DEPTH, 2 * H_B)], COL_DECAY)
    ssm_prm = _param_row([ssm_a_log.reshape(DEPTH, 2 * H_D), ssm_dt_bias.reshape(DEPTH, 2 * H_D)], COL_DT, width=256)
    ssm_prm = ssm_prm.at[:, 2, :].set(jnp.repeat(ssm_d, P_D, axis=1))

    tabs_a = _tile_lanes(_rope_tables(len_s, DQK_A, 256), 256 // DQK_A)
    tabs_c = _rope_tables(len_s, ROPE_C, 128)
    tabs_small = _pad_lanes(tabs_c, 128)
    tabs_q = _tile_lanes(tabs_c, 128 // ROPE_C)

    past_kr = jnp.pad(cache_mla_krope, ((0, 0), (0, 0), (0, 0), (0, 128 - ROPE_C)))

    def layer(x, l, batch, seq_len, is_latent):
        (a_qkv, b_qkv, b_gate, c_q, d_z, d_xbc, c_kv, small) = _inproj(
            x, mods, w_in_p, l, seq_len, is_latent, tabs_a + tabs_small if is_latent else None)
        if is_latent:
            past_a = (cache_diff_k[:, l].reshape(batch, past_len, 256), cache_diff_v[:, l].reshape(batch, past_len, 256))
            past_c = (cache_mla_ckv[:, l], past_kr[:, l])
            s0_b, s0_d = state_delta[:, l], state_ssm[:, l]
        else:
            past_a = past_c = s0_b = s0_d = None
        o_a = _diff_attn(a_qkv, batch, seq_len, lam_rows[l], diff_g[l], past_a)
        o_c, ckv_n = _mla_attn(c_q, c_kv, small, batch, seq_len,
                               (mla_q_norm[l][None, :], mla_kv_norm[l][None, :]), (w_uq_p, w_uk16, w_uv16), l,
                               past_c, tabs_q if is_latent else None)
        o_bf, o_bb, st_b = _deltanet(b_qkv, small, batch, seq_len, dn_conv[l], dn_prm[l], s0_b)
        y_df, y_db, st_d = _ssd(d_xbc, small, batch, seq_len, ssm_conv_w[l], ssm_conv_b[l][None, :], ssm_prm[l], s0_d)
        x = _outproj_ffn(x, mods, o_a, o_bf, o_bb, b_gate, o_c, y_df, y_db, d_z,
                         (dn_g[l], ss_g[l], w_out16, ln1_g[l][None, :], ln1_b[l][None, :], w_ff1_16, w_ff2_16,
                          ln2_g[l][None, :], ln2_b[l][None, :]), l, seq_len, is_latent)
        return x, (a_qkv, st_b, ckv_n, small, st_d)

    xp = x_prompt.reshape(nb_p * len_p, D_MODEL)
    xs = x_sample.reshape(nb_s * len_s, D_MODEL)
    ctx = ([], [], [], [], [], [])
    for l in range(DEPTH):
        xp, (a_qkv, st_b, ckv_n, small, st_d) = layer(xp, l, nb_p, len_p, False)
        xs, _ = layer(xs, l, nb_s, len_s, True)
        ctx[0].append(a_qkv[:, 256:512].reshape(nb_p, len_p, H_A, 2, DQK_A))
        ctx[1].append(a_qkv[:, 512:768].reshape(nb_p, len_p, H_A, DV_A))
        ctx[2].append(st_b)
        ctx[3].append(ckv_n.reshape(nb_p, len_p, KV_RANK))
        ctx[4].append(small[:, COL_KR:COL_KR + ROPE_C].reshape(nb_p, len_p, ROPE_C))
        ctx[5].append(st_d)
    return (xp.reshape(nb_p, len_p, D_MODEL), xs.reshape(nb_s, len_s, D_MODEL)) + tuple(
        jnp.stack(t, axis=1) for t in ctx)
```

```python
import functools
import math

import jax
import jax.numpy as jnp
import numpy as np
from jax import lax
from jax.experimental import pallas as pl
from jax.experimental.pallas import tpu as pltpu

F32 = jnp.float32
BF16 = jnp.bfloat16

D_MODEL = 1024
DEPTH = 4
GRID_W = 64
GROUP_W = 256
H_A, DV_A, DQK_A = 4, 64, 32
H_B, DK_B, DV_B = 4, 64, 64
H_C, NOPE_C, ROPE_C, V_C = 4, 64, 32, 64
Q_RANK, KV_RANK = 256, 128
H_D, P_D, N_D, G_D = 4, 64, 128, 2
CHUNK = 64
D_FF = 4 * D_MODEL
ROPE_BASE = 10000.0
EPS = 1e-6
ALPHA = (2 * DEPTH) ** 0.25
LOG2E = 1.4426950408889634

SPLIT_NAMES = ("a_q", "a_k", "a_v", "b_qkv", "b_beta", "b_decay", "b_gate",
               "c_q", "c_kv", "c_kr", "d_z", "d_xbc", "d_dt")
SPLIT_SIZES = (256, 256, 256, 768, 8, 8, 256, 256, 128, 32, 256, 768, 8)
OUT_GROUPS = (("a_q", "a_k", "a_v"), ("b_qkv",), ("b_gate",), ("c_q",), ("d_z",), ("d_xbc",),
              ("c_kv",), ("c_kr", "b_beta", "b_decay", "d_dt"))
OUT_WIDTHS = (768, 768, 256, 256, 256, 768, 128, 128)
IN_COLS_PAD = sum(OUT_WIDTHS)
COL_KR, COL_BETA, COL_DECAY, COL_DT = 0, 32, 40, 48

TILE = 256
TQ = 512
TB = 512
VMEM_LIMIT = 56 * 1024 * 1024


def _cparams(n_axes):
    return pltpu.CompilerParams(dimension_semantics=("arbitrary",) * n_axes,
                                vmem_limit_bytes=VMEM_LIMIT)


def _layer_spec(stacked, layer, **kwargs):
    zeros = (0,) * (stacked.ndim - 1)
    return pl.BlockSpec((None,) + stacked.shape[1:], lambda *_: (layer,) + zeros, **kwargs)


def _dot(a, b):
    return jnp.dot(a, b, preferred_element_type=F32)


def _dot_nt(a, b):
    return lax.dot_general(a, b, (((1,), (1,)), ((), ())), preferred_element_type=F32)


def _split2(x):
    hi = x.astype(BF16)
    return hi, (x - hi.astype(F32)).astype(BF16)


def _sigmoid(x):
    return 1.0 / (1.0 + jnp.exp(-x))


def _silu(x):
    return x * _sigmoid(x)


def _softplus(x):
    return jnp.maximum(x, 0.0) + jnp.log1p(jnp.exp(-jnp.abs(x)))


def _rms_groups(x, width):
    outs = []
    for s in range(0, x.shape[1], width):
        xg = x[:, s:s + width]
        ms = jnp.mean(xg * xg, axis=-1, keepdims=True)
        outs.append(xg * lax.rsqrt(ms + EPS))
    return outs[0] if len(outs) == 1 else jnp.concatenate(outs, axis=1)


def _layer_norm(x, g, b):
    mu = jnp.mean(x, axis=-1, keepdims=True)
    xc = x - mu
    var = jnp.mean(xc * xc, axis=-1, keepdims=True)
    return xc * lax.rsqrt(var + EPS) * g + b


def _rope(x, cos, sin_lo, sin_hi):
    w = x.shape[1]
    return x * cos + pltpu.roll(x, w - 8, 1) * sin_lo + pltpu.roll(x, 8, 1) * sin_hi


def _mod_kernel(c_ref, w_ref, b_ref, o_ref):
    c = c_ref[...]
    o_ref[...] = _dot(_silu(c).astype(BF16), w_ref[...].astype(BF16)) + b_ref[...]


def _mod_call(cvec, w_mod, b_mod):
    nblk = 4
    wcol = 6 * D_MODEL // nblk
    return pl.pallas_call(
        _mod_kernel,
        grid=(DEPTH, nblk),
        in_specs=[pl.BlockSpec((8, D_MODEL), lambda l, j: (0, 0)),
                  pl.BlockSpec((None, D_MODEL, wcol), lambda l, j: (l, 0, j)),
                  pl.BlockSpec((None, 1, wcol), lambda l, j: (l, 0, j))],
        out_specs=pl.BlockSpec((None, 8, wcol), lambda l, j: (l, 0, j)),
        out_shape=jax.ShapeDtypeStruct((DEPTH, 8, 6 * D_MODEL), F32),
        compiler_params=_cparams(2),
        name="mod",
    )(cvec, w_mod, b_mod.reshape(DEPTH, 1, 6 * D_MODEL))


def _inproj_kernel(has_rope, *refs):
    if has_rope:
        (x_ref, mod_ref, w_ref, cos_ref, slo_ref, shi_ref, cos_s_ref, slo_s_ref, shi_s_ref,
         oa, obq, obg, ocq, odz, odx, ockv, osm) = refs
    else:
        x_ref, mod_ref, w_ref, oa, obq, obg, ocq, odz, odx, ockv, osm = refs
    x = x_ref[...]
    h = (x * (1.0 + mod_ref[1:2, :]) + mod_ref[0:1, :]).astype(BF16)
    outs = (oa, obq, obg, ocq, odz, odx, ockv, osm)
    off = 0
    for idx, (o_ref, wd) in enumerate(zip(outs, OUT_WIDTHS)):
        y = _dot(h, w_ref[:, off:off + wd])
        off += wd
        if has_rope and idx == 0:
            cos, slo, shi = cos_ref[...], slo_ref[...], shi_ref[...]
            o_ref[:, 0:256] = _rope(y[:, 0:256], cos, slo, shi)
            o_ref[:, 256:512] = _rope(y[:, 256:512], cos, slo, shi)
            o_ref[:, 512:768] = y[:, 512:768]
        elif has_rope and idx == 7:
            o_ref[...] = _rope(y, cos_s_ref[...], slo_s_ref[...], shi_s_ref[...])
        else:
            o_ref[...] = y


def _inproj(x, mods, w_in_p, layer, seq_len, per_batch_mod, rope_tabs):
    t = x.shape[0]
    nb_seq = seq_len // TB
    has_rope = rope_tabs is not None
    mod_idx = (lambda i: (layer, 1 + i // nb_seq, 0, 0)) if per_batch_mod else (lambda i: (layer, 0, 0, 0))
    in_specs = [pl.BlockSpec((TB, D_MODEL), lambda i: (i, 0)),
                pl.BlockSpec((None, None, 6, D_MODEL), mod_idx),
                _layer_spec(w_in_p, layer)]
    args = [x, mods, w_in_p]
    if has_rope:
        for tab in rope_tabs:
            in_specs.append(pl.BlockSpec((TB, tab.shape[1]), lambda i: (i % nb_seq, 0)))
            args.append(tab)
    return pl.pallas_call(
        functools.partial(_inproj_kernel, has_rope),
        grid=(t // TB,),
        in_specs=in_specs,
        out_specs=[pl.BlockSpec((TB, wd), lambda i: (i, 0)) for wd in OUT_WIDTHS],
        out_shape=[jax.ShapeDtypeStruct((t, wd), F32) for wd in OUT_WIDTHS],
        compiler_params=_cparams(1),
        name="inproj",
    )(*args)


VT_ROWS = 80


def _key_halves(k_ref):
    half = k_ref.shape[0] // 2
    return [slice(0, half), slice(half, k_ref.shape[0])]


def _scores_t(k_ref, col0, qt16):
    return [_dot(k_ref[rows, col0:col0 + 128], qt16) for rows in _key_halves(k_ref)]


def _softmax_pv_t(s, k_ref, vt_ref, h):
    tile_max = [jnp.max(x.reshape(x.shape[0] // 128, 128, x.shape[1]), axis=0) for x in s]
    m = jnp.max(jnp.maximum(tile_max[0], tile_max[1]), axis=0, keepdims=True)
    acc = None
    for rows, x in zip(_key_halves(k_ref), s):
        part = _dot(vt_ref[h, :, rows], jnp.exp2(x - m).astype(BF16))
        acc = part if acc is None else acc + part
    return acc[0:64] / acc[64:65]


def _attend_all_t(k_ref, vt_ref, jobs):
    outs = []
    s_next = _scores_t(k_ref, jobs[0][0], jobs[0][1])
    for i, (_, _, h) in enumerate(jobs):
        s = s_next
        if i + 1 < len(jobs):
            s_next = _scores_t(k_ref, jobs[i + 1][0], jobs[i + 1][1])
        outs.append(_softmax_pv_t(s, k_ref, vt_ref, h))
    return outs


def _diff_attn_kernel(has_past, nch_own, tq, *refs):
    if has_past:
        q_ref, k_ref, v_ref, pk_ref, pv_ref, lam_ref, g_ref, o_ref, k_scr, vt_scr = refs
    else:
        q_ref, k_ref, v_ref, lam_ref, g_ref, o_ref, k_scr, vt_scr = refs

    @pl.when(pl.program_id(1) == 0)
    def _prep():
        def put(c, k, v):
            rows = slice(c * TILE, (c + 1) * TILE)
            k_scr[rows, :] = k.astype(BF16)
            vt = v.T
            for h in range(H_A):
                vt_scr[h, 0:DV_A, rows] = vt[h * DV_A:(h + 1) * DV_A, :].astype(BF16)

        vt_scr[:, DV_A:VT_ROWS, :] = jnp.ones((H_A, VT_ROWS - DV_A, vt_scr.shape[2]), BF16)
        for c in range(nch_own):
            put(c, k_ref[c * TILE:(c + 1) * TILE, :], v_ref[c * TILE:(c + 1) * TILE, :])
        if has_past:
            put(nch_own, pk_ref[...], pv_ref[...])

    lp = lam_ref[...]
    lam_init = lam_ref[4:5, 0:1]
    lam = (jnp.exp(jnp.sum(lp[0:1] * lp[1:2], axis=-1, keepdims=True))
           - jnp.exp(jnp.sum(lp[2:3] * lp[3:4], axis=-1, keepdims=True)) + lam_init)
    qt = (q_ref[...] * (DQK_A ** -0.5 * LOG2E)).T
    slot = lax.broadcasted_iota(jnp.int32, (128, tq), 0) // DQK_A
    jobs = []
    for h in range(H_A):
        grp = (h * 2 * DQK_A) // 128
        qg = qt[grp * 128:(grp + 1) * 128, :]
        for t in range(2):
            qm = jnp.where(slot == (h * 2 + t) % (128 // DQK_A), qg, 0.0).astype(BF16)
            jobs.append((grp * 128, qm, h))
    outs = _attend_all_t(k_scr, vt_scr, jobs)
    heads = [outs[2 * h] - lam * outs[2 * h + 1] for h in range(H_A)]
    o = jnp.concatenate(heads, axis=0).T
    o_ref[...] = _rms_groups(o, DV_A) * g_ref[...] * (1.0 - lam_init)


def _diff_attn(a_qkv, batch, seq_len, lam_p, g_row, past):
    tq = min(TQ, seq_len)
    nq = seq_len // tq
    nch_own = seq_len // TILE
    has_past = past is not None
    n_keys = seq_len + (TILE if has_past else 0)
    in_specs = [pl.BlockSpec((tq, 256), lambda b, i: (b * nq + i, 0)),
                pl.BlockSpec((seq_len, 256), lambda b, i: (b, 1)),
                pl.BlockSpec((seq_len, 256), lambda b, i: (b, 2))]
    args = [a_qkv, a_qkv, a_qkv]
    if has_past:
        in_specs += [pl.BlockSpec((None, TILE, 256), lambda b, i: (b, 0, 0))] * 2
        args += list(past)
    in_specs += [pl.BlockSpec((8, DQK_A), lambda b, i: (0, 0)), pl.BlockSpec((1, 256), lambda b, i: (0, 0))]
    args += [lam_p, g_row]
    return pl.pallas_call(
        functools.partial(_diff_attn_kernel, has_past, nch_own, tq),
        grid=(batch, nq),
        in_specs=in_specs,
        out_specs=pl.BlockSpec((tq, 256), lambda b, i: (b * nq + i, 0)),
        out_shape=jax.ShapeDtypeStruct((batch * seq_len, 256), F32),
        scratch_shapes=[pltpu.VMEM((n_keys, 256), BF16), pltpu.VMEM((H_A, VT_ROWS, n_keys), BF16)],
        compiler_params=_cparams(2),
        name="diff_attn",
    )(*args)


def _mla_kernel(has_past, has_rope, nch_own, tq, *refs):
    refs = list(refs)
    cq_ref, ckv_ref, sm_ref = refs[:3]
    pos = 3
    if has_past:
        pckv_ref, pkr_ref = refs[pos:pos + 2]
        pos += 2
    qn_ref, kvn_ref, wuq_ref, wuk_ref, wuv_ref = refs[pos:pos + 5]
    pos += 5
    if has_rope:
        cos_ref, slo_ref, shi_ref = refs[pos:pos + 3]
        pos += 3
    o_ref, ckv_out = refs[pos:pos + 2]
    k_scr, vt_scr = refs[pos + 2:]

    def put(c, ckv, kr):
        rows = slice(c * TILE, (c + 1) * TILE)
        c16 = ckv.astype(BF16)
        kn = _dot(c16, wuk_ref[...])
        vt = _dot(c16, wuv_ref[...]).T
        lane = lax.broadcasted_iota(jnp.int32, kr.shape, 1)
        kr16 = jnp.where(lane < ROPE_C, kr, 0.0)[:, 0:128 - NOPE_C].astype(BF16)
        for h in range(H_C):
            k_scr[rows, h * 128:h * 128 + NOPE_C] = kn[:, h * NOPE_C:(h + 1) * NOPE_C].astype(BF16)
            k_scr[rows, h * 128 + NOPE_C:(h + 1) * 128] = kr16
            vt_scr[h, 0:V_C, rows] = vt[h * V_C:(h + 1) * V_C, :].astype(BF16)

    @pl.when(pl.program_id(1) == 0)
    def _prep():
        vt_scr[:, V_C:VT_ROWS, :] = jnp.ones((H_C, VT_ROWS - V_C, vt_scr.shape[2]), BF16)
        for c in range(nch_own):
            rows = slice(c * TILE, (c + 1) * TILE)
            ckv = _rms_groups(ckv_ref[rows, :], KV_RANK) * kvn_ref[...]
            ckv_out[rows, :] = ckv
            put(c, ckv, sm_ref[rows, :])
        if has_past:
            put(nch_own, pckv_ref[...], pkr_ref[...])

    cqn = (_rms_groups(cq_ref[...], Q_RANK) * qn_ref[...]).astype(BF16)
    qa = _dot(cqn, wuq_ref[...])
    q_nope = qa[:, 0:256]
    q_rope = qa[:, 256:384]
    if has_rope:
        q_rope = _rope(q_rope, cos_ref[...], slo_ref[...], shi_ref[...])
    scale = (NOPE_C + ROPE_C) ** -0.5 * LOG2E
    zpad = jnp.zeros((tq, 128 - NOPE_C - ROPE_C), F32)
    jobs = []
    for h in range(H_C):
        qh = jnp.concatenate([q_nope[:, h * NOPE_C:(h + 1) * NOPE_C],
                              q_rope[:, h * ROPE_C:(h + 1) * ROPE_C], zpad], axis=1) * scale
        jobs.append((h * 128, qh.T.astype(BF16), h))
    o_ref[...] = jnp.concatenate(_attend_all_t(k_scr, vt_scr, jobs), axis=0).T


def _mla_attn(c_q, c_kv, small, batch, seq_len, norms, stacked_wts, layer, past, rope_tabs):
    tq = min(TQ, seq_len)
    nq = seq_len // tq
    nch_own = seq_len // TILE
    has_past = past is not None
    has_rope = rope_tabs is not None
    n_keys = seq_len + (TILE if has_past else 0)
    const = lambda b, i: (0, 0)
    in_specs = [pl.BlockSpec((tq, Q_RANK), lambda b, i: (b * nq + i, 0)),
                pl.BlockSpec((seq_len, KV_RANK), lambda b, i: (b, 0)),
                pl.BlockSpec((seq_len, 128), lambda b, i: (b, 0))]
    args = [c_q, c_kv, small]
    if has_past:
        in_specs += [pl.BlockSpec((None, TILE, 128), lambda b, i: (b, 0, 0))] * 2
        args += list(past)
    in_specs += [pl.BlockSpec((1, Q_RANK), const), pl.BlockSpec((1, KV_RANK), const)]
    in_specs += [_layer_spec(w, layer) for w in stacked_wts]
    args += list(norms) + list(stacked_wts)
    if has_rope:
        in_specs += [pl.BlockSpec((tq, 128), lambda b, i: (i, 0))] * 3
        args += list(rope_tabs)
    return pl.pallas_call(
        functools.partial(_mla_kernel, has_past, has_rope, nch_own, tq),
        grid=(batch, nq),
        in_specs=in_specs,
        out_specs=[pl.BlockSpec((tq, 256), lambda b, i: (b * nq + i, 0)),
                   pl.BlockSpec((seq_len, KV_RANK), lambda b, i: (b, 0))],
        out_shape=[jax.ShapeDtypeStruct((batch * seq_len, 256), F32),
                   jax.ShapeDtypeStruct((batch * seq_len, KV_RANK), F32)],
        scratch_shapes=[pltpu.VMEM((n_keys, 512), BF16), pltpu.VMEM((H_C, VT_ROWS, n_keys), BF16)],
        compiler_params=_cparams(2),
        name="mla_attn",
    )(*args)


def _conv3(main, prev8, next8, w, has_prev, has_next):
    n = main.shape[0]
    row = lax.broadcasted_iota(jnp.int32, main.shape, 0)
    before = jnp.where(has_prev, prev8[7:8, :], 0.0)
    after = jnp.where(has_next, next8[0:1, :], 0.0)
    xm = jnp.where(row == 0, before, pltpu.roll(main, 1, 0))
    xp = jnp.where(row == n - 1, after, pltpu.roll(main, n - 1, 0))
    return xm * w[0:1, :] + main * w[1:2, :] + xp * w[2:3, :]


def _tile_specs(batch, seq_len, widths, reverse):
    nt = seq_len // TILE
    last8 = batch * seq_len // 8 - 1

    def tile(b, i):
        return b * nt + ((nt - 1 - i) if reverse else i)

    specs = []
    for width, halos in widths:
        specs.append(pl.BlockSpec((TILE, width), lambda b, i: (tile(b, i), 0)))
        if halos:
            specs.append(pl.BlockSpec((8, width), lambda b, i: (jnp.maximum(tile(b, i) * (TILE // 8) - 1, 0), 0)))
            specs.append(pl.BlockSpec((8, width), lambda b, i: (jnp.minimum((tile(b, i) + 1) * (TILE // 8), last8), 0)))
    return specs


def _deltanet_kernel(nt, has_state, *refs):
    refs = list(refs)
    tiles = (refs[0:4], refs[4:8])
    w_ref, prm_ref, cf_ref, cb_ref, ce_ref = refs[8:13]
    n_in = 14 if has_state else 13
    of_ref, ob_ref, sfin_ref, s_scr = refs[n_in:]
    step = pl.program_id(1)
    n_chunks = TILE // CHUNK

    @pl.when(step == 0)
    def _init():
        s_scr[...] = jnp.zeros(s_scr.shape, F32)
        if has_state:
            for d in range(2):
                for h in range(H_B):
                    s_scr[d, h * DK_B:(h + 1) * DK_B, h * DV_B:(h + 1) * DV_B] = refs[13][d, h]

    block16 = cb_ref[CB_SAME]
    block = cf_ref[CF_SAME]

    def expand(x, e16):
        hi, lo = _split2(x)
        return _dot(hi, e16) + _dot(lo, e16)

    def head_sums(x):
        hi, lo = _split2(x)
        return _dot(hi, block16) + _dot(lo, block16)

    chains, dirs = [], []
    shared = None
    for dirn in (0, 1):
        main_ref, prev_ref, next_ref, sm_ref = tiles[dirn]
        tpos = (nt - 1 - step) if dirn == 1 else step
        if shared is None or nt > 1:
            qkv = _silu(_conv3(main_ref[...], prev_ref[...], next_ref[...], w_ref[...], tpos > 0, tpos < nt - 1))
            q, k, v = qkv[:, 0:256], qkv[:, 256:512], qkv[:, 512:768]
            qn = q * lax.rsqrt(head_sums(q * q) + EPS) * (DK_B ** -0.5)
            kn = k * lax.rsqrt(head_sums(k * k) + EPS)
            knt16 = kn.T.astype(BF16)
            sm = sm_ref[...]
            lane = lax.broadcasted_iota(jnp.int32, sm.shape, 1)
            beta = _sigmoid(sm)
            g = -jnp.exp(prm_ref[0:1, :]) * _softplus(sm + prm_ref[1:2, :])
            g = jnp.where(jnp.logical_and(lane >= COL_DECAY, lane < COL_DECAY + 2 * H_B), g, 0.0)
            shared = (qn, kn, knt16, v, beta, g)
        qn, kn, knt16, v, beta, g = shared
        beta_x = expand(beta, ce_ref[CE_BETA + dirn])
        g_hi, g_lo = _split2(expand(g, ce_ref[CE_DECAY + dirn]))
        gc = _dot(cb_ref[dirn], g_hi) + _dot(cb_ref[dirn], g_lo)
        gt = _dot(block16, g_hi) + _dot(block16, g_lo)
        gc_t = gc.T
        egc = jnp.exp(gc)
        kb = kn * beta_x
        dirs.append(dict(vb=v * beta_x, kbe=kb * egc, qd=qn * egc, egl=jnp.exp(gt),
                         kdt=(kn * jnp.exp(gt - gc)).T.astype(BF16), s=s_scr[dirn], o=[None] * n_chunks))
        kb16, qn16 = kb.astype(BF16), qn.astype(BF16)
        for h in range(H_B):
            lanes16 = cb_ref[CB_LANES + h]
            dec = jnp.exp((gc[:, h * DK_B:h * DK_B + 1] - gc_t[h * DK_B:h * DK_B + 1, :]) + cf_ref[dirn])
            a = _dot(kb16 * lanes16, knt16) * dec * cf_ref[CF_OFFDIAG]
            chains.append(dict(dirn=dirn, h=h, a=a, r=-(a * cf_ref[CF_LEVEL]),
                               qk16=(_dot(qn16 * lanes16, knt16) * dec).astype(BF16)))

    for lvl in range(1, CHUNK.bit_length() - 1):
        for ch in chains:
            e = ch["a"] * cf_ref[CF_LEVEL + lvl]
            x = e + _dot(ch["r"].astype(BF16), e.astype(BF16))
            ch["r"] = ch["r"] - x - _dot(x.astype(BF16), ch["r"].astype(BF16))

    for dirn, dd in enumerate(dirs):
        mine = [ch for ch in chains if ch["dirn"] == dirn]
        vb16, kbe16 = dd["vb"].astype(BF16), dd["kbe"].astype(BF16)
        u, w = dd["vb"], dd["kbe"]
        for ch in mine:
            r16, lanes16 = ch["r"].astype(BF16), cb_ref[CB_LANES + ch["h"]]
            u = u + _dot(r16, vb16 * lanes16)
            w = w + _dot(r16, kbe16 * lanes16)
        u16, w16 = u.astype(BF16), w.astype(BF16)
        qku, qkw = None, None
        for ch in mine:
            lanes16 = cb_ref[CB_LANES + ch["h"]]
            a_u, a_w = _dot(ch["qk16"], u16 * lanes16), _dot(ch["qk16"], w16 * lanes16)
            qku = a_u if qku is None else qku + a_u
            qkw = a_w if qkw is None else qkw + a_w
        dd["qku"] = qku
        dd["qp16"] = (dd["qd"] - qkw).astype(BF16)
        dd["wu16"] = jnp.concatenate([w16, u16], axis=1)

    for ci in range(n_chunks):
        for dirn, dd in enumerate(dirs):
            c = (n_chunks - 1 - ci) if dirn == 1 else ci
            rows = slice(c * CHUNK, (c + 1) * CHUNK)
            s16 = dd["s"].astype(BF16)
            dd["o"][c] = _dot(dd["qp16"][rows], s16) + dd["qku"][rows]
            pb = _dot(dd["kdt"] * cb_ref[CB_LANES + c], dd["wu16"])
            p16 = (pb[:, 0:256] * block).astype(BF16)
            dd["s"] = dd["s"] * dd["egl"][c * CHUNK:c * CHUNK + 1, :] - _dot(p16, s16) + pb[:, 256:512] * block

    for dirn, (dd, o_ref) in enumerate(zip(dirs, (of_ref, ob_ref))):
        s_scr[dirn] = dd["s"]
        o_ref[...] = jnp.concatenate(dd["o"], axis=0)

    @pl.when(step == nt - 1)
    def _fin():
        for d in range(2):
            for h in range(H_B):
                sfin_ref[d, h] = s_scr[d, h * DK_B:(h + 1) * DK_B, h * DV_B:(h + 1) * DV_B]


CF_NINF, CF_OFFDIAG, CF_LEVEL, CF_SAME = 0, 2, 3, 9
CB_RUN, CB_SAME, CB_LANES = 0, 2, 3
CE_BETA, CE_DECAY, CE_DT = 0, 2, 4


def _scan_consts():
    i = np.arange(TILE)[:, None]
    j = np.arange(TILE)[None, :]
    same = (i // CHUNK) == (j // CHUNK)
    run = [same & (i >= j), same & (i <= j)]
    levels = [((i >> (l + 1)) == (j >> (l + 1))) & ((i >> l) != (j >> l)) for l in range(CHUNK.bit_length() - 1)]
    cf = np.stack([np.where(m, 0.0, -np.inf) for m in run] + [i != j] + levels + [same]).astype(np.float32)
    cb = np.stack(run + [same] + [np.broadcast_to(j // CHUNK == c, (TILE, TILE)) for c in range(TILE // CHUNK)])
    rows = np.arange(128)[:, None]
    ce = np.stack([rows == col0 + d * H_B + j // DK_B for col0 in (COL_BETA, COL_DECAY, COL_DT) for d in range(2)])
    return jnp.asarray(cf), jnp.asarray(cb, BF16), jnp.asarray(ce, BF16)


def _deltanet(b_qkv, small, batch, seq_len, conv_w, prm, s0):
    nt = seq_len // TILE
    has_state = s0 is not None
    const = lambda b, i: (0, 0)
    const3 = lambda b, i: (0, 0, 0)
    consts = _scan_consts()
    widths = ((768, True), (128, False))
    in_specs = (_tile_specs(batch, seq_len, widths, False) + _tile_specs(batch, seq_len, widths, True)
                + [pl.BlockSpec((3, 768), const), pl.BlockSpec((8, 128), const)]
                + [pl.BlockSpec(cst.shape, const3) for cst in consts])
    args = [b_qkv, b_qkv, b_qkv, small] * 2 + [conv_w, prm] + list(consts)
    state_spec = pl.BlockSpec((None, 2, H_B, DK_B, DV_B), lambda b, i: (b, 0, 0, 0, 0))
    if has_state:
        in_specs.append(state_spec)
        args.append(s0)
    return pl.pallas_call(
        functools.partial(_deltanet_kernel, nt, has_state),
        grid=(batch, nt),
        in_specs=in_specs,
        out_specs=[pl.BlockSpec((TILE, 256), lambda b, i: (b * nt + i, 0)),
                   pl.BlockSpec((TILE, 256), lambda b, i: (b * nt + nt - 1 - i, 0)), state_spec],
        out_shape=[jax.ShapeDtypeStruct((batch * seq_len, 256), F32)] * 2
        + [jax.ShapeDtypeStruct((batch, 2, H_B, DK_B, DV_B), F32)],
        scratch_shapes=[pltpu.VMEM((2, H_B * DK_B, H_B * DV_B), F32)],
        compiler_params=_cparams(2),
        name="deltanet",
    )(*args)


def _ssd_kernel(nt, has_state, *refs):
    refs = list(refs)
    tiles = (refs[0:4], refs[4:8])
    w_ref, bias_ref, prm_ref, cf_ref, cb_ref, ce_ref = refs[8:14]
    n_in = 15 if has_state else 14
    yf_ref, yb_ref, sfin_ref, s_scr = refs[n_in:]
    step = pl.program_id(1)
    n_chunks = TILE // CHUNK
    glanes = (H_D // G_D) * P_D

    @pl.when(step == 0)
    def _init():
        if has_state:
            for d in range(2):
                s_scr[d] = jnp.concatenate([refs[14][d, h] for h in range(H_D)], axis=0).T
        else:
            s_scr[...] = jnp.zeros(s_scr.shape, F32)

    block16 = cb_ref[CB_SAME]

    def expand(x, e16):
        hi, lo = _split2(x)
        return _dot(hi, e16) + _dot(lo, e16)

    def group_lanes(per_group):
        return jnp.concatenate([m[:, g * glanes:(g + 1) * glanes] for g, m in enumerate(per_group)], axis=1)

    dirs = []
    shared = None
    for dirn in (0, 1):
        main_ref, prev_ref, next_ref, sm_ref = tiles[dirn]
        tpos = (nt - 1 - step) if dirn == 1 else step
        if shared is None or nt > 1:
            xbc = _silu(_conv3(main_ref[...], prev_ref[...], next_ref[...], w_ref[...], tpos > 0, tpos < nt - 1)
                        + bias_ref[...])
            x = xbc[:, 0:256]
            bm16, cm16 = xbc[:, 256:512].astype(BF16), xbc[:, 512:768].astype(BF16)
            cbs = [_dot_nt(cm16[:, g * N_D:(g + 1) * N_D], bm16[:, g * N_D:(g + 1) * N_D]) for g in range(G_D)]
            bmt16 = xbc[:, 256:512].T.astype(BF16)
            sm = sm_ref[...]
            lane = lax.broadcasted_iota(jnp.int32, sm.shape, 1)
            dt = _softplus(sm + prm_ref[1:2, 0:128])
            da = jnp.where(jnp.logical_and(lane >= COL_DT, lane < COL_DT + 2 * H_D),
                           dt * (-jnp.exp(prm_ref[0:1, 0:128])), 0.0)
            shared = (x, cm16, cbs, bmt16, dt, da)
        x, cm16, cbs, bmt16, dt, da = shared
        e16 = ce_ref[CE_DT + dirn]
        a_hi, a_lo = _split2(expand(da, e16))
        ac = _dot(cb_ref[dirn], a_hi) + _dot(cb_ref[dirn], a_lo)
        at = _dot(block16, a_hi) + _dot(block16, a_lo)
        ac_t = ac.T
        xdt = x * expand(dt, e16)
        xdt16 = xdt.astype(BF16)
        y = None
        for h in range(H_D):
            seg = jnp.exp((ac[:, h * P_D:h * P_D + 1] - ac_t[h * P_D:h * P_D + 1, :]) + cf_ref[dirn])
            part = _dot((cbs[h * P_D // glanes] * seg).astype(BF16), xdt16 * cb_ref[CB_LANES + h])
            y = part if y is None else y + part
        if dirn == 0:
            y = y + prm_ref[2:3, :] * x
        dirs.append(dict(y=y, eac=jnp.exp(ac), eat=jnp.exp(at), xdd16=(xdt * jnp.exp(at - ac)).astype(BF16),
                         cm16=cm16, bmt16=bmt16, s=s_scr[dirn], rows=[None] * n_chunks))

    for ci in range(n_chunks):
        for dirn, dd in enumerate(dirs):
            c = (n_chunks - 1 - ci) if dirn == 1 else ci
            rows = slice(c * CHUNK, (c + 1) * CHUNK)
            s16 = dd["s"].astype(BF16)
            inter = group_lanes([_dot(dd["cm16"][rows, g * N_D:(g + 1) * N_D], s16) for g in range(G_D)])
            dd["rows"][c] = dd["y"][rows] + inter * dd["eac"][rows]
            upd = group_lanes([_dot(dd["bmt16"][g * N_D:(g + 1) * N_D, :] * cb_ref[CB_LANES + c, 0:N_D, :],
                                    dd["xdd16"]) for g in range(G_D)])
            dd["s"] = dd["s"] * dd["eat"][c * CHUNK:c * CHUNK + 1, :] + upd

    for dirn, (dd, y_ref) in enumerate(zip(dirs, (yf_ref, yb_ref))):
        s_scr[dirn] = dd["s"]
        y_ref[...] = jnp.concatenate(dd["rows"], axis=0)

    @pl.when(step == nt - 1)
    def _fin():
        for d in range(2):
            st = s_scr[d].T
            for h in range(H_D):
                sfin_ref[d, h] = st[h * P_D:(h + 1) * P_D, :]


def _ssd(d_xbc, small, batch, seq_len, conv_w, conv_b, prm, s0):
    nt = seq_len // TILE
    has_state = s0 is not None
    const = lambda b, i: (0, 0)
    const3 = lambda b, i: (0, 0, 0)
    consts = _scan_consts()
    widths = ((768, True), (128, False))
    in_specs = (_tile_specs(batch, seq_len, widths, False) + _tile_specs(batch, seq_len, widths, True)
                + [pl.BlockSpec((3, 768), const), pl.BlockSpec((1, 768), const), pl.BlockSpec((8, 256), const)]
                + [pl.BlockSpec(cst.shape, const3) for cst in consts])
    args = [d_xbc, d_xbc, d_xbc, small] * 2 + [conv_w, conv_b, prm] + list(consts)
    state_spec = pl.BlockSpec((None, 2, H_D, P_D, N_D), lambda b, i: (b, 0, 0, 0, 0))
    if has_state:
        in_specs.append(state_spec)
        args.append(s0)
    return pl.pallas_call(
        functools.partial(_ssd_kernel, nt, has_state),
        grid=(batch, nt),
        in_specs=in_specs,
        out_specs=[pl.BlockSpec((TILE, 256), lambda b, i: (b * nt + i, 0)),
                   pl.BlockSpec((TILE, 256), lambda b, i: (b * nt + nt - 1 - i, 0)), state_spec],
        out_shape=[jax.ShapeDtypeStruct((batch * seq_len, 256), F32)] * 2
        + [jax.ShapeDtypeStruct((batch, 2, H_D, P_D, N_D), F32)],
        scratch_shapes=[pltpu.VMEM((2, N_D, H_D * P_D), F32)],
        compiler_params=_cparams(2),
        name="ssd",
    )(*args)


def _outproj_ffn_kernel(x_ref, mod_ref, oa_ref, obf_ref, obb_ref, bg_ref, oc_ref, ydf_ref, ydb_ref, dz_ref,
                        dng_ref, ssg_ref, wo_ref, l1g_ref, l1b_ref, w1_ref, w2_ref, l2g_ref, l2b_ref, o_ref):
    x = x_ref[...]
    o_b = _rms_groups(obf_ref[...] + obb_ref[...], DV_B) * dng_ref[...] * _silu(bg_ref[...])
    y_d = (ydf_ref[...] + ydb_ref[...]) * _silu(dz_ref[...])
    o_d = _rms_groups(y_d, GROUP_W // G_D) * ssg_ref[...]
    mixed = None
    for idx, part in enumerate((oa_ref[...], o_b, oc_ref[...], o_d)):
        y = _dot(part.astype(BF16), wo_ref[idx * GROUP_W:(idx + 1) * GROUP_W, :])
        mixed = y if mixed is None else mixed + y
    x1 = _layer_norm(ALPHA * x + mod_ref[2:3, :] * mixed, l1g_ref[...], l1b_ref[...])
    h16 = (x1 * (1.0 + mod_ref[4:5, :]) + mod_ref[3:4, :]).astype(BF16)
    ff = None
    fcol = 1024
    for j in range(D_FF // fcol):
        a = jnp.square(jnp.maximum(_dot(h16, w1_ref[:, j * fcol:(j + 1) * fcol]), 0.0)).astype(BF16)
        y = _dot(a, w2_ref[j * fcol:(j + 1) * fcol, :])
        ff = y if ff is None else ff + y
    o_ref[...] = _layer_norm(ALPHA * x1 + mod_ref[5:6, :] * ff, l2g_ref[...], l2b_ref[...])


def _outproj_ffn(x, mods, o_a, o_bf, o_bb, b_gate, o_c, y_df, y_db, d_z, wts, layer, seq_len, per_batch_mod):
    t = x.shape[0]
    tb = min(TB, seq_len)
    nb_seq = seq_len // tb
    mod_idx = (lambda i: (layer, 1 + i // nb_seq, 0, 0)) if per_batch_mod else (lambda i: (layer, 0, 0, 0))
    row = lambda wd: pl.BlockSpec((tb, wd), lambda i: (i, 0))
    const = lambda i: (0, 0)
    single = pl.Buffered(1)
    dn_g, ss_g, w_out16, l1g, l1b, w_ff1_16, w_ff2_16, l2g, l2b = wts
    in_specs = [row(D_MODEL), pl.BlockSpec((None, None, 6, D_MODEL), mod_idx),
                row(256), row(256), row(256), row(256), row(256), row(256), row(256), row(256),
                pl.BlockSpec((1, 256), const), pl.BlockSpec((1, 256), const),
                _layer_spec(w_out16, layer, pipeline_mode=single),
                pl.BlockSpec((1, D_MODEL), const), pl.BlockSpec((1, D_MODEL), const),
                _layer_spec(w_ff1_16, layer, pipeline_mode=single),
                _layer_spec(w_ff2_16, layer, pipeline_mode=single),
                pl.BlockSpec((1, D_MODEL), const), pl.BlockSpec((1, D_MODEL), const)]
    return pl.pallas_call(
        _outproj_ffn_kernel,
        grid=(t // tb,),
        in_specs=in_specs,
        out_specs=row(D_MODEL),
        out_shape=jax.ShapeDtypeStruct((t, D_MODEL), F32),
        compiler_params=_cparams(1),
        name="outproj_ffn",
    )(x, mods, o_a, o_bf, o_bb, b_gate, o_c, y_df, y_db, d_z, *wts)


def _rope_tables(length, dim, width):
    rows = length // GRID_W
    row_pos = jnp.repeat(jnp.arange(rows, dtype=F32), GRID_W)
    col_pos = jnp.tile(jnp.arange(GRID_W, dtype=F32), rows)
    half = dim // 2
    inv_freq = ROPE_BASE ** (-jnp.arange(0, half, 2, dtype=F32) / half)
    ang_r = row_pos[:, None] * inv_freq
    ang_c = col_pos[:, None] * inv_freq
    ang = jnp.concatenate([ang_r, ang_r, ang_c, ang_c], axis=-1)
    cos, sin = jnp.cos(ang), jnp.sin(ang)
    first = (jnp.arange(dim) % (dim // 2)) < (dim // 4)
    sin_lo = jnp.where(first, -sin, 0.0)
    sin_hi = jnp.where(first, 0.0, sin)
    return cos, sin_lo, sin_hi


def _tile_lanes(tabs, reps):
    return tuple(jnp.tile(t, (1, reps)) for t in tabs)


def _pad_lanes(tabs, width):
    cos, lo, hi = tabs
    pad = width - cos.shape[1]
    return (jnp.pad(cos, ((0, 0), (0, pad)), constant_values=1.0), jnp.pad(lo, ((0, 0), (0, pad))),
            jnp.pad(hi, ((0, 0), (0, pad))))


def _permute_w_in(w_in):
    offs = {}
    o = 0
    for name, size in zip(SPLIT_NAMES, SPLIT_SIZES):
        offs[name] = (o, size)
        o += size
    cols = []
    for names, width in zip(OUT_GROUPS, OUT_WIDTHS):
        used = 0
        for n in names:
            s, size = offs[n]
            cols.append(w_in[:, :, s:s + size])
            used += size
        if used < width:
            cols.append(jnp.zeros(w_in.shape[:2] + (width - used,), w_in.dtype))
    return jnp.concatenate(cols, axis=-1).astype(BF16)


def _param_row(vals, col0, width=128, rows=8):
    out = jnp.zeros((DEPTH, rows, width), F32)
    for r, v in enumerate(vals):
        out = out.at[:, r, col0:col0 + v.shape[1]].set(v.astype(F32))
    return out


def kernel(x_prompt, x_sample, cache_diff_k, cache_diff_v, state_delta, cache_mla_ckv, cache_mla_krope, state_ssm, c, c_ctx, w_mod, b_mod, w_in, diff_lam, diff_norm, dn_conv, dn_a_log, dn_dt_bias, dn_norm, mla_q_norm, mla_kv_norm, mla_w_uq, mla_w_uk, mla_w_uv, ssm_conv_w, ssm_conv_b, ssm_a_log, ssm_dt_bias, ssm_d, ssm_norm, w_out, ln1_g, ln1_b, ln2_g, ln2_b, w_ff1, w_ff2):
    nb_p, len_p = x_prompt.shape[:2]
    nb_s, len_s = x_sample.shape[:2]
    past_len = cache_diff_k.shape[2]
    assert past_len == TILE and nb_s + 1 <= 8

    cvec = jnp.concatenate([c_ctx[None, :], c, jnp.zeros((8 - 1 - nb_s, D_MODEL), F32)], axis=0)
    mods = _mod_call(cvec, w_mod, b_mod).reshape(DEPTH, 8, 6, D_MODEL)

    w_in_p = _permute_w_in(w_in)
    w_uq = mla_w_uq.reshape(DEPTH, Q_RANK, H_C, NOPE_C + ROPE_C)
    w_uq_p = jnp.concatenate([w_uq[..., :NOPE_C].reshape(DEPTH, Q_RANK, H_C * NOPE_C),
                              w_uq[..., NOPE_C:].reshape(DEPTH, Q_RANK, H_C * ROPE_C)], axis=-1).astype(BF16)
    w_uk16, w_uv16 = mla_w_uk.astype(BF16), mla_w_uv.astype(BF16)
    w_out16, w_ff1_16, w_ff2_16 = w_out.astype(BF16), w_ff1.astype(BF16), w_ff2.astype(BF16)
    diff_g = jnp.tile(diff_norm, (1, H_A))[:, None, :]
    lam_init = jnp.asarray([0.8 - 0.6 * math.exp(-0.3 * l) for l in range(DEPTH)], F32)
    lam_rows = jnp.concatenate([diff_lam, jnp.broadcast_to(lam_init[:, None, None], (DEPTH, 4, DQK_A))], axis=1)
    dn_g = jnp.tile(dn_norm, (1, H_B))[:, None, :]
    ss_g = ssm_norm[:, None, :]
    dn_prm = _param_row([dn_a_log.reshape(DEPTH, 2 * H_B), dn_dt_bias.reshape(DEPTH, 2 * H_B)], COL_DECAY)
    ssm_prm = _param_row([ssm_a_log.reshape(DEPTH, 2 * H_D), ssm_dt_bias.reshape(DEPTH, 2 * H_D)], COL_DT, width=256)
    ssm_prm = ssm_prm.at[:, 2, :].set(jnp.repeat(ssm_d, P_D, axis=1))

    tabs_a = _tile_lanes(_rope_tables(len_s, DQK_A, 256), 256 // DQK_A)
    tabs_c = _rope_tables(len_s, ROPE_C, 128)
    tabs_small = _pad_lanes(tabs_c, 128)
    tabs_q = _tile_lanes(tabs_c, 128 // ROPE_C)

    past_kr = jnp.pad(cache_mla_krope, ((0, 0), (0, 0), (0, 0), (0, 128 - ROPE_C)))

    def layer(x, l, batch, seq_len, is_latent):
        (a_qkv, b_qkv, b_gate, c_q, d_z, d_xbc, c_kv, small) = _inproj(
            x, mods, w_in_p, l, seq_len, is_latent, tabs_a + tabs_small if is_latent else None)
        if is_latent:
            past_a = (cache_diff_k[:, l].reshape(batch, past_len, 256), cache_diff_v[:, l].reshape(batch, past_len, 256))
            past_c = (cache_mla_ckv[:, l], past_kr[:, l])
            s0_b, s0_d = state_delta[:, l], state_ssm[:, l]
        else:
            past_a = past_c = s0_b = s0_d = None
        o_a = _diff_attn(a_qkv, batch, seq_len, lam_rows[l], diff_g[l], past_a)
        o_c, ckv_n = _mla_attn(c_q, c_kv, small, batch, seq_len,
                               (mla_q_norm[l][None, :], mla_kv_norm[l][None, :]), (w_uq_p, w_uk16, w_uv16), l,
                               past_c, tabs_q if is_latent else None)
        o_bf, o_bb, st_b = _deltanet(b_qkv, small, batch, seq_len, dn_conv[l], dn_prm[l], s0_b)
        y_df, y_db, st_d = _ssd(d_xbc, small, batch, seq_len, ssm_conv_w[l], ssm_conv_b[l][None, :], ssm_prm[l], s0_d)
        x = _outproj_ffn(x, mods, o_a, o_bf, o_bb, b_gate, o_c, y_df, y_db, d_z,
                         (dn_g[l], ss_g[l], w_out16, ln1_g[l][None, :], ln1_b[l][None, :], w_ff1_16, w_ff2_16,
                          ln2_g[l][None, :], ln2_b[l][None, :]), l, seq_len, is_latent)
        return x, (a_qkv, st_b, ckv_n, small, st_d)

    xp = x_prompt.reshape(nb_p * len_p, D_MODEL)
    xs = x_sample.reshape(nb_s * len_s, D_MODEL)
    ctx = ([], [], [], [], [], [])
    for l in range(DEPTH):
        xp, (a_qkv, st_b, ckv_n, small, st_d) = layer(xp, l, nb_p, len_p, False)
        xs, _ = layer(xs, l, nb_s, len_s, True)
        ctx[0].append(a_qkv[:, 256:512].reshape(nb_p, len_p, H_A, 2, DQK_A))
        ctx[1].append(a_qkv[:, 512:768].reshape(nb_p, len_p, H_A, DV_A))
        ctx[2].append(st_b)
        ctx[3].append(ckv_n.reshape(nb_p, len_p, KV_RANK))
        ctx[4].append(small[:, COL_KR:COL_KR + ROPE_C].reshape(nb_p, len_p, ROPE_C))
        ctx[5].append(st_d)
    return (xp.reshape(nb_p, len_p, D_MODEL), xs.reshape(nb_s, len_s, D_MODEL)) + tuple(
        jnp.stack(t, axis=1) for t in ctx)
```

```python
import functools
import math

import jax
import jax.numpy as jnp
import numpy as np
from jax import lax
from jax.experimental import pallas as pl
from jax.experimental.pallas import tpu as pltpu

F32 = jnp.float32
BF16 = jnp.bfloat16

D_MODEL = 1024
DEPTH = 4
GRID_W = 64
GROUP_W = 256
H_A, DV_A, DQK_A = 4, 64, 32
H_B, DK_B, DV_B = 4, 64, 64
H_C, NOPE_C, ROPE_C, V_C = 4, 64, 32, 64
Q_RANK, KV_RANK = 256, 128
H_D, P_D, N_D, G_D = 4, 64, 128, 2
CHUNK = 64
D_FF = 4 * D_MODEL
ROPE_BASE = 10000.0
EPS = 1e-6
ALPHA = (2 * DEPTH) ** 0.25
LOG2E = 1.4426950408889634

SPLIT_NAMES = ("a_q", "a_k", "a_v", "b_qkv", "b_beta", "b_decay", "b_gate",
               "c_q", "c_kv", "c_kr", "d_z", "d_xbc", "d_dt")
SPLIT_SIZES = (256, 256, 256, 768, 8, 8, 256, 256, 128, 32, 256, 768, 8)
OUT_GROUPS = (("a_q", "a_k", "a_v"), ("b_qkv",), ("b_gate",), ("c_q",), ("d_z",), ("d_xbc",),
              ("c_kv",), ("c_kr", "b_beta", "b_decay", "d_dt"))
OUT_WIDTHS = (768, 768, 256, 256, 256, 768, 128, 128)
IN_COLS_PAD = sum(OUT_WIDTHS)
COL_KR, COL_BETA, COL_DECAY, COL_DT = 0, 32, 40, 48

TILE = 256
TQ = 512
TB = 512
VMEM_LIMIT = 56 * 1024 * 1024


def _cparams(n_axes):
    return pltpu.CompilerParams(dimension_semantics=("arbitrary",) * n_axes,
                                vmem_limit_bytes=VMEM_LIMIT)


def _layer_spec(stacked, layer, **kwargs):
    zeros = (0,) * (stacked.ndim - 1)
    return pl.BlockSpec((None,) + stacked.shape[1:], lambda *_: (layer,) + zeros, **kwargs)


def _dot(a, b):
    return jnp.dot(a, b, preferred_element_type=F32)


def _dot_nt(a, b):
    return lax.dot_general(a, b, (((1,), (1,)), ((), ())), preferred_element_type=F32)


def _split2(x):
    hi = x.astype(BF16)
    return hi, (x - hi.astype(F32)).astype(BF16)


def _sigmoid(x):
    return 1.0 / (1.0 + jnp.exp(-x))


def _silu(x):
    return x * _sigmoid(x)


def _softplus(x):
    return jnp.maximum(x, 0.0) + jnp.log1p(jnp.exp(-jnp.abs(x)))


def _rms_groups(x, width):
    outs = []
    for s in range(0, x.shape[1], width):
        xg = x[:, s:s + width]
        ms = jnp.mean(xg * xg, axis=-1, keepdims=True)
        outs.append(xg * lax.rsqrt(ms + EPS))
    return outs[0] if len(outs) == 1 else jnp.concatenate(outs, axis=1)


def _layer_norm(x, g, b):
    mu = jnp.mean(x, axis=-1, keepdims=True)
    xc = x - mu
    var = jnp.mean(xc * xc, axis=-1, keepdims=True)
    return xc * lax.rsqrt(var + EPS) * g + b


def _rope(x, cos, sin_lo, sin_hi):
    w = x.shape[1]
    return x * cos + pltpu.roll(x, w - 8, 1) * sin_lo + pltpu.roll(x, 8, 1) * sin_hi


def _mod_kernel(c_ref, w_ref, b_ref, o_ref):
    c = c_ref[...]
    o_ref[...] = _dot(_silu(c).astype(BF16), w_ref[...].astype(BF16)) + b_ref[...]


def _mod_call(cvec, w_mod, b_mod):
    nblk = 4
    wcol = 6 * D_MODEL // nblk
    return pl.pallas_call(
        _mod_kernel,
        grid=(DEPTH, nblk),
        in_specs=[pl.BlockSpec((8, D_MODEL), lambda l, j: (0, 0)),
                  pl.BlockSpec((None, D_MODEL, wcol), lambda l, j: (l, 0, j)),
                  pl.BlockSpec((None, 1, wcol), lambda l, j: (l, 0, j))],
        out_specs=pl.BlockSpec((None, 8, wcol), lambda l, j: (l, 0, j)),
        out_shape=jax.ShapeDtypeStruct((DEPTH, 8, 6 * D_MODEL), F32),
        compiler_params=_cparams(2),
        name="mod",
    )(cvec, w_mod, b_mod.reshape(DEPTH, 1, 6 * D_MODEL))


def _inproj_kernel(has_rope, *refs):
    if has_rope:
        (x_ref, mod_ref, w_ref, cos_ref, slo_ref, shi_ref, cos_s_ref, slo_s_ref, shi_s_ref,
         oa, obq, obg, ocq, odz, odx, ockv, osm) = refs
    else:
        x_ref, mod_ref, w_ref, oa, obq, obg, ocq, odz, odx, ockv, osm = refs
    x = x_ref[...]
    h = (x * (1.0 + mod_ref[1:2, :]) + mod_ref[0:1, :]).astype(BF16)
    outs = (oa, obq, obg, ocq, odz, odx, ockv, osm)
    off = 0
    for idx, (o_ref, wd) in enumerate(zip(outs, OUT_WIDTHS)):
        y = _dot(h, w_ref[:, off:off + wd])
        off += wd
        if has_rope and idx == 0:
            cos, slo, shi = cos_ref[...], slo_ref[...], shi_ref[...]
            o_ref[:, 0:256] = _rope(y[:, 0:256], cos, slo, shi)
            o_ref[:, 256:512] = _rope(y[:, 256:512], cos, slo, shi)
            o_ref[:, 512:768] = y[:, 512:768]
        elif has_rope and idx == 7:
            o_ref[...] = _rope(y, cos_s_ref[...], slo_s_ref[...], shi_s_ref[...])
        else:
            o_ref[...] = y


def _inproj(x, mods, w_in_p, layer, seq_len, per_batch_mod, rope_tabs):
    t = x.shape[0]
    nb_seq = seq_len // TB
    has_rope = rope_tabs is not None
    mod_idx = (lambda i: (layer, 1 + i // nb_seq, 0, 0)) if per_batch_mod else (lambda i: (layer, 0, 0, 0))
    in_specs = [pl.BlockSpec((TB, D_MODEL), lambda i: (i, 0)),
                pl.BlockSpec((None, None, 6, D_MODEL), mod_idx),
                _layer_spec(w_in_p, layer)]
    args = [x, mods, w_in_p]
    if has_rope:
        for tab in rope_tabs:
            in_specs.append(pl.BlockSpec((TB, tab.shape[1]), lambda i: (i % nb_seq, 0)))
            args.append(tab)
    return pl.pallas_call(
        functools.partial(_inproj_kernel, has_rope),
        grid=(t // TB,),
        in_specs=in_specs,
        out_specs=[pl.BlockSpec((TB, wd), lambda i: (i, 0)) for wd in OUT_WIDTHS],
        out_shape=[jax.ShapeDtypeStruct((t, wd), F32) for wd in OUT_WIDTHS],
        compiler_params=_cparams(1),
        name="inproj",
    )(*args)


VT_ROWS = 80


def _key_halves(k_ref):
    half = k_ref.shape[0] // 2
    return [slice(0, half), slice(half, k_ref.shape[0])]


def _scores_t(k_ref, col0, qt16):
    return [_dot(k_ref[rows, col0:col0 + 128], qt16) for rows in _key_halves(k_ref)]


def _softmax_pv_t(s, k_ref, vt_ref, h):
    tile_max = [jnp.max(x.reshape(x.shape[0] // 128, 128, x.shape[1]), axis=0) for x in s]
    m = jnp.max(jnp.maximum(tile_max[0], tile_max[1]), axis=0, keepdims=True)
    acc = None
    for rows, x in zip(_key_halves(k_ref), s):
        part = _dot(vt_ref[h, :, rows], jnp.exp2(x - m).astype(BF16))
        acc = part if acc is None else acc + part
    return acc[0:64] / acc[64:65]


def _attend_all_t(k_ref, vt_ref, jobs):
    outs = []
    s_next = _scores_t(k_ref, jobs[0][0], jobs[0][1])
    for i, (_, _, h) in enumerate(jobs):
        s = s_next
        if i + 1 < len(jobs):
            s_next = _scores_t(k_ref, jobs[i + 1][0], jobs[i + 1][1])
        outs.append(_softmax_pv_t(s, k_ref, vt_ref, h))
    return outs


def _diff_attn_kernel(has_past, nch_own, tq, *refs):
    if has_past:
        q_ref, k_ref, v_ref, pk_ref, pv_ref, lam_ref, g_ref, o_ref, k_scr, vt_scr = refs
    else:
        q_ref, k_ref, v_ref, lam_ref, g_ref, o_ref, k_scr, vt_scr = refs

    @pl.when(pl.program_id(1) == 0)
    def _prep():
        def put(c, k, v):
            rows = slice(c * TILE, (c + 1) * TILE)
            k_scr[rows, :] = k.astype(BF16)
            vt = v.T
            for h in range(H_A):
                vt_scr[h, 0:DV_A, rows] = vt[h * DV_A:(h + 1) * DV_A, :].astype(BF16)

        vt_scr[:, DV_A:VT_ROWS, :] = jnp.ones((H_A, VT_ROWS - DV_A, vt_scr.shape[2]), BF16)
        for c in range(nch_own):
            put(c, k_ref[c * TILE:(c + 1) * TILE, :], v_ref[c * TILE:(c + 1) * TILE, :])
        if has_past:
            put(nch_own, pk_ref[...], pv_ref[...])

    lp = lam_ref[...]
    lam_init = lam_ref[4:5, 0:1]
    lam = (jnp.exp(jnp.sum(lp[0:1] * lp[1:2], axis=-1, keepdims=True))
           - jnp.exp(jnp.sum(lp[2:3] * lp[3:4], axis=-1, keepdims=True)) + lam_init)
    qt = (q_ref[...] * (DQK_A ** -0.5 * LOG2E)).T
    slot = lax.broadcasted_iota(jnp.int32, (128, tq), 0) // DQK_A
    jobs = []
    for h in range(H_A):
        grp = (h * 2 * DQK_A) // 128
        qg = qt[grp * 128:(grp + 1) * 128, :]
        for t in range(2):
            qm = jnp.where(slot == (h * 2 + t) % (128 // DQK_A), qg, 0.0).astype(BF16)
            jobs.append((grp * 128, qm, h))
    outs = _attend_all_t(k_scr, vt_scr, jobs)
    heads = [outs[2 * h] - lam * outs[2 * h + 1] for h in range(H_A)]
    o = jnp.concatenate(heads, axis=0).T
    o_ref[...] = _rms_groups(o, DV_A) * g_ref[...] * (1.0 - lam_init)


def _diff_attn(a_qkv, batch, seq_len, lam_p, g_row, past):
    tq = min(TQ, seq_len)
    nq = seq_len // tq
    nch_own = seq_len // TILE
    has_past = past is not None
    n_keys = seq_len + (TILE if has_past else 0)
    in_specs = [pl.BlockSpec((tq, 256), lambda b, i: (b * nq + i, 0)),
                pl.BlockSpec((seq_len, 256), lambda b, i: (b, 1)),
                pl.BlockSpec((seq_len, 256), lambda b, i: (b, 2))]
    args = [a_qkv, a_qkv, a_qkv]
    if has_past:
        in_specs += [pl.BlockSpec((None, TILE, 256), lambda b, i: (b, 0, 0))] * 2
        args += list(past)
    in_specs += [pl.BlockSpec((8, DQK_A), lambda b, i: (0, 0)), pl.BlockSpec((1, 256), lambda b, i: (0, 0))]
    args += [lam_p, g_row]
    return pl.pallas_call(
        functools.partial(_diff_attn_kernel, has_past, nch_own, tq),
        grid=(batch, nq),
        in_specs=in_specs,
        out_specs=pl.BlockSpec((tq, 256), lambda b, i: (b * nq + i, 0)),
        out_shape=jax.ShapeDtypeStruct((batch * seq_len, 256), F32),
        scratch_shapes=[pltpu.VMEM((n_keys, 256), BF16), pltpu.VMEM((H_A, VT_ROWS, n_keys), BF16)],
        compiler_params=_cparams(2),
        name="diff_attn",
    )(*args)


def _mla_kernel(has_past, has_rope, nch_own, tq, *refs):
    refs = list(refs)
    cq_ref, ckv_ref, sm_ref = refs[:3]
    pos = 3
    if has_past:
        pckv_ref, pkr_ref = refs[pos:pos + 2]
        pos += 2
    qn_ref, kvn_ref, wuq_ref, wuk_ref, wuv_ref = refs[pos:pos + 5]
    pos += 5
    if has_rope:
        cos_ref, slo_ref, shi_ref = refs[pos:pos + 3]
        pos += 3
    o_ref, ckv_out = refs[pos:pos + 2]
    k_scr, vt_scr = refs[pos + 2:]

    def put(c, ckv, kr):
        rows = slice(c * TILE, (c + 1) * TILE)
        c16 = ckv.astype(BF16)
        kn = _dot(c16, wuk_ref[...])
        vt = _dot(c16, wuv_ref[...]).T
        lane = lax.broadcasted_iota(jnp.int32, kr.shape, 1)
        kr16 = jnp.where(lane < ROPE_C, kr, 0.0)[:, 0:128 - NOPE_C].astype(BF16)
        for h in range(H_C):
            k_scr[rows, h * 128:h * 128 + NOPE_C] = kn[:, h * NOPE_C:(h + 1) * NOPE_C].astype(BF16)
            k_scr[rows, h * 128 + NOPE_C:(h + 1) * 128] = kr16
            vt_scr[h, 0:V_C, rows] = vt[h * V_C:(h + 1) * V_C, :].astype(BF16)

    @pl.when(pl.program_id(1) == 0)
    def _prep():
        vt_scr[:, V_C:VT_ROWS, :] = jnp.ones((H_C, VT_ROWS - V_C, vt_scr.shape[2]), BF16)
        for c in range(nch_own):
            rows = slice(c * TILE, (c + 1) * TILE)
            ckv = _rms_groups(ckv_ref[rows, :], KV_RANK) * kvn_ref[...]
            ckv_out[rows, :] = ckv
            put(c, ckv, sm_ref[rows, :])
        if has_past:
            put(nch_own, pckv_ref[...], pkr_ref[...])

    cqn = (_rms_groups(cq_ref[...], Q_RANK) * qn_ref[...]).astype(BF16)
    qa = _dot(cqn, wuq_ref[...])
    q_nope = qa[:, 0:256]
    q_rope = qa[:, 256:384]
    if has_rope:
        q_rope = _rope(q_rope, cos_ref[...], slo_ref[...], shi_ref[...])
    scale = (NOPE_C + ROPE_C) ** -0.5 * LOG2E
    zpad = jnp.zeros((tq, 128 - NOPE_C - ROPE_C), F32)
    jobs = []
    for h in range(H_C):
        qh = jnp.concatenate([q_nope[:, h * NOPE_C:(h + 1) * NOPE_C],
                              q_rope[:, h * ROPE_C:(h + 1) * ROPE_C], zpad], axis=1) * scale
        jobs.append((h * 128, qh.T.astype(BF16), h))
    o_ref[...] = jnp.concatenate(_attend_all_t(k_scr, vt_scr, jobs), axis=0).T


def _mla_attn(c_q, c_kv, small, batch, seq_len, norms, stacked_wts, layer, past, rope_tabs):
    tq = min(TQ, seq_len)
    nq = seq_len // tq
    nch_own = seq_len // TILE
    has_past = past is not None
    has_rope = rope_tabs is not None
    n_keys = seq_len + (TILE if has_past else 0)
    const = lambda b, i: (0, 0)
    in_specs = [pl.BlockSpec((tq, Q_RANK), lambda b, i: (b * nq + i, 0)),
                pl.BlockSpec((seq_len, KV_RANK), lambda b, i: (b, 0)),
                pl.BlockSpec((seq_len, 128), lambda b, i: (b, 0))]
    args = [c_q, c_kv, small]
    if has_past:
        in_specs += [pl.BlockSpec((None, TILE, 128), lambda b, i: (b, 0, 0))] * 2
        args += list(past)
    in_specs += [pl.BlockSpec((1, Q_RANK), const), pl.BlockSpec((1, KV_RANK), const)]
    in_specs += [_layer_spec(w, layer) for w in stacked_wts]
    args += list(norms) + list(stacked_wts)
    if has_rope:
        in_specs += [pl.BlockSpec((tq, 128), lambda b, i: (i, 0))] * 3
        args += list(rope_tabs)
    return pl.pallas_call(
        functools.partial(_mla_kernel, has_past, has_rope, nch_own, tq),
        grid=(batch, nq),
        in_specs=in_specs,
        out_specs=[pl.BlockSpec((tq, 256), lambda b, i: (b * nq + i, 0)),
                   pl.BlockSpec((seq_len, KV_RANK), lambda b, i: (b, 0))],
        out_shape=[jax.ShapeDtypeStruct((batch * seq_len, 256), F32),
                   jax.ShapeDtypeStruct((batch * seq_len, KV_RANK), F32)],
        scratch_shapes=[pltpu.VMEM((n_keys, 512), BF16), pltpu.VMEM((H_C, VT_ROWS, n_keys), BF16)],
        compiler_params=_cparams(2),
        name="mla_attn",
    )(*args)


def _conv3(main, prev8, next8, w, has_prev, has_next):
    n = main.shape[0]
    row = lax.broadcasted_iota(jnp.int32, main.shape, 0)
    before = jnp.where(has_prev, prev8[7:8, :], 0.0)
    after = jnp.where(has_next, next8[0:1, :], 0.0)
    xm = jnp.where(row == 0, before, pltpu.roll(main, 1, 0))
    xp = jnp.where(row == n - 1, after, pltpu.roll(main, n - 1, 0))
    return xm * w[0:1, :] + main * w[1:2, :] + xp * w[2:3, :]


def _tile_specs(batch, seq_len, widths, reverse):
    nt = seq_len // TILE
    last8 = batch * seq_len // 8 - 1

    def tile(b, i):
        return b * nt + ((nt - 1 - i) if reverse else i)

    specs = []
    for width, halos in widths:
        specs.append(pl.BlockSpec((TILE, width), lambda b, i: (tile(b, i), 0)))
        if halos:
            specs.append(pl.BlockSpec((8, width), lambda b, i: (jnp.maximum(tile(b, i) * (TILE // 8) - 1, 0), 0)))
            specs.append(pl.BlockSpec((8, width), lambda b, i: (jnp.minimum((tile(b, i) + 1) * (TILE // 8), last8), 0)))
    return specs


def _deltanet_kernel(nt, has_state, *refs):
    refs = list(refs)
    tiles = (refs[0:4], refs[4:8])
    w_ref, prm_ref, cf_ref, cb_ref, ce_ref = refs[8:13]
    n_in = 14 if has_state else 13
    of_ref, ob_ref, sfin_ref, s_scr = refs[n_in:]
    step = pl.program_id(1)
    n_chunks = TILE // CHUNK

    @pl.when(step == 0)
    def _init():
        s_scr[...] = jnp.zeros(s_scr.shape, F32)
        if has_state:
            for d in range(2):
                for h in range(H_B):
                    s_scr[d, h * DK_B:(h + 1) * DK_B, h * DV_B:(h + 1) * DV_B] = refs[13][d, h]

    block16 = cb_ref[CB_SAME]
    block = cf_ref[CF_SAME]

    def expand(x, e16):
        hi, lo = _split2(x)
        return _dot(hi, e16) + _dot(lo, e16)

    def head_sums(x):
        return _dot(x.astype(BF16), block16)

    chains, dirs = [], []
    shared = None
    for dirn in (0, 1):
        main_ref, prev_ref, next_ref, sm_ref = tiles[dirn]
        tpos = (nt - 1 - step) if dirn == 1 else step
        if shared is None or nt > 1:
            qkv = _silu(_conv3(main_ref[...], prev_ref[...], next_ref[...], w_ref[...], tpos > 0, tpos < nt - 1))
            q, k, v = qkv[:, 0:256], qkv[:, 256:512], qkv[:, 512:768]
            qn = q * lax.rsqrt(head_sums(q * q) + EPS) * (DK_B ** -0.5)
            kn = k * lax.rsqrt(head_sums(k * k) + EPS)
            knt16 = kn.T.astype(BF16)
            sm = sm_ref[...]
            lane = lax.broadcasted_iota(jnp.int32, sm.shape, 1)
            beta = _sigmoid(sm)
            g = -jnp.exp(prm_ref[0:1, :]) * _softplus(sm + prm_ref[1:2, :])
            g = jnp.where(jnp.logical_and(lane >= COL_DECAY, lane < COL_DECAY + 2 * H_B), g, 0.0)
            shared = (qn, kn, knt16, v, beta, g)
        qn, kn, knt16, v, beta, g = shared
        beta_x = expand(beta, ce_ref[CE_BETA + dirn])
        g_hi, g_lo = _split2(expand(g, ce_ref[CE_DECAY + dirn]))
        gc = _dot(cb_ref[dirn], g_hi) + _dot(cb_ref[dirn], g_lo)
        gt = _dot(block16, g_hi) + _dot(block16, g_lo)
        gc_t = gc.T
        egc = jnp.exp(gc)
        kb = kn * beta_x
        dirs.append(dict(vb=v * beta_x, kbe=kb * egc, qd=qn * egc, egl=jnp.exp(gt),
                         kdt=(kn * jnp.exp(gt - gc)).T.astype(BF16), s=s_scr[dirn], o=[None] * n_chunks))
        kb16, qn16 = kb.astype(BF16), qn.astype(BF16)
        for h in range(H_B):
            lanes16 = cb_ref[CB_LANES + h]
            dec = jnp.exp((gc[:, h * DK_B:h * DK_B + 1] - gc_t[h * DK_B:h * DK_B + 1, :]) + cf_ref[dirn])
            a = _dot(kb16 * lanes16, knt16) * dec * cf_ref[CF_OFFDIAG]
            chains.append(dict(dirn=dirn, h=h, a=a, r=-(a * cf_ref[CF_LEVEL]),
                               qk16=(_dot(qn16 * lanes16, knt16) * dec).astype(BF16)))

    for lvl in range(1, CHUNK.bit_length() - 1):
        for ch in chains:
            e = ch["a"] * cf_ref[CF_LEVEL + lvl]
            x = e + _dot(ch["r"].astype(BF16), e.astype(BF16))
            ch["r"] = ch["r"] - x - _dot(x.astype(BF16), ch["r"].astype(BF16))

    for dirn, dd in enumerate(dirs):
        mine = [ch for ch in chains if ch["dirn"] == dirn]
        vb16, kbe16 = dd["vb"].astype(BF16), dd["kbe"].astype(BF16)
        u, w = dd["vb"], dd["kbe"]
        for ch in mine:
            r16, lanes16 = ch["r"].astype(BF16), cb_ref[CB_LANES + ch["h"]]
            u = u + _dot(r16, vb16 * lanes16)
            w = w + _dot(r16, kbe16 * lanes16)
        u16, w16 = u.astype(BF16), w.astype(BF16)
        qku, qkw = None, None
        for ch in mine:
            lanes16 = cb_ref[CB_LANES + ch["h"]]
            a_u, a_w = _dot(ch["qk16"], u16 * lanes16), _dot(ch["qk16"], w16 * lanes16)
            qku = a_u if qku is None else qku + a_u
            qkw = a_w if qkw is None else qkw + a_w
        dd["qku"] = qku
        dd["qp16"] = (dd["qd"] - qkw).astype(BF16)
        dd["wu16"] = jnp.concatenate([w16, u16], axis=1)

    for ci in range(n_chunks):
        for dirn, dd in enumerate(dirs):
            c = (n_chunks - 1 - ci) if dirn == 1 else ci
            rows = slice(c * CHUNK, (c + 1) * CHUNK)
            s16 = dd["s"].astype(BF16)
            dd["o"][c] = _dot(dd["qp16"][rows], s16) + dd["qku"][rows]
            pb = _dot(dd["kdt"] * cb_ref[CB_LANES + c], dd["wu16"])
            p16 = (pb[:, 0:256] * block).astype(BF16)
            dd["s"] = dd["s"] * dd["egl"][c * CHUNK:c * CHUNK + 1, :] - _dot(p16, s16) + pb[:, 256:512] * block

    for dirn, (dd, o_ref) in enumerate(zip(dirs, (of_ref, ob_ref))):
        s_scr[dirn] = dd["s"]
        o_ref[...] = jnp.concatenate(dd["o"], axis=0)

    @pl.when(step == nt - 1)
    def _fin():
        for d in range(2):
            for h in range(H_B):
                sfin_ref[d, h] = s_scr[d, h * DK_B:(h + 1) * DK_B, h * DV_B:(h + 1) * DV_B]


CF_NINF, CF_OFFDIAG, CF_LEVEL, CF_SAME = 0, 2, 3, 9
CB_RUN, CB_SAME, CB_LANES = 0, 2, 3
CE_BETA, CE_DECAY, CE_DT = 0, 2, 4


def _scan_consts():
    i = np.arange(TILE)[:, None]
    j = np.arange(TILE)[None, :]
    same = (i // CHUNK) == (j // CHUNK)
    run = [same & (i >= j), same & (i <= j)]
    levels = [((i >> (l + 1)) == (j >> (l + 1))) & ((i >> l) != (j >> l)) for l in range(CHUNK.bit_length() - 1)]
    cf = np.stack([np.where(m, 0.0, -np.inf) for m in run] + [i != j] + levels + [same]).astype(np.float32)
    cb = np.stack(run + [same] + [np.broadcast_to(j // CHUNK == c, (TILE, TILE)) for c in range(TILE // CHUNK)])
    rows = np.arange(128)[:, None]
    ce = np.stack([rows == col0 + d * H_B + j // DK_B for col0 in (COL_BETA, COL_DECAY, COL_DT) for d in range(2)])
    return jnp.asarray(cf), jnp.asarray(cb, BF16), jnp.asarray(ce, BF16)


def _deltanet(b_qkv, small, batch, seq_len, conv_w, prm, s0):
    nt = seq_len // TILE
    has_state = s0 is not None
    const = lambda b, i: (0, 0)
    const3 = lambda b, i: (0, 0, 0)
    consts = _scan_consts()
    widths = ((768, True), (128, False))
    in_specs = (_tile_specs(batch, seq_len, widths, False) + _tile_specs(batch, seq_len, widths, True)
                + [pl.BlockSpec((3, 768), const), pl.BlockSpec((8, 128), const)]
                + [pl.BlockSpec(cst.shape, const3) for cst in consts])
    args = [b_qkv, b_qkv, b_qkv, small] * 2 + [conv_w, prm] + list(consts)
    state_spec = pl.BlockSpec((None, 2, H_B, DK_B, DV_B), lambda b, i: (b, 0, 0, 0, 0))
    if has_state:
        in_specs.append(state_spec)
        args.append(s0)
    return pl.pallas_call(
        functools.partial(_deltanet_kernel, nt, has_state),
        grid=(batch, nt),
        in_specs=in_specs,
        out_specs=[pl.BlockSpec((TILE, 256), lambda b, i: (b * nt + i, 0)),
                   pl.BlockSpec((TILE, 256), lambda b, i: (b * nt + nt - 1 - i, 0)), state_spec],
        out_shape=[jax.ShapeDtypeStruct((batch * seq_len, 256), F32)] * 2
        + [jax.ShapeDtypeStruct((batch, 2, H_B, DK_B, DV_B), F32)],
        scratch_shapes=[pltpu.VMEM((2, H_B * DK_B, H_B * DV_B), F32)],
        compiler_params=_cparams(2),
        name="deltanet",
    )(*args)


def _ssd_kernel(nt, has_state, *refs):
    refs = list(refs)
    tiles = (refs[0:4], refs[4:8])
    w_ref, bias_ref, prm_ref, cf_ref, cb_ref, ce_ref = refs[8:14]
    n_in = 15 if has_state else 14
    yf_ref, yb_ref, sfin_ref, s_scr = refs[n_in:]
    step = pl.program_id(1)
    n_chunks = TILE // CHUNK
    glanes = (H_D // G_D) * P_D

    @pl.when(step == 0)
    def _init():
        if has_state:
            for d in range(2):
                s_scr[d] = jnp.concatenate([refs[14][d, h] for h in range(H_D)], axis=0).T
        else:
            s_scr[...] = jnp.zeros(s_scr.shape, F32)

    block16 = cb_ref[CB_SAME]

    def expand(x, e16):
        hi, lo = _split2(x)
        return _dot(hi, e16) + _dot(lo, e16)

    def group_lanes(per_group):
        return jnp.concatenate([m[:, g * glanes:(g + 1) * glanes] for g, m in enumerate(per_group)], axis=1)

    dirs = []
    shared = None
    for dirn in (0, 1):
        main_ref, prev_ref, next_ref, sm_ref = tiles[dirn]
        tpos = (nt - 1 - step) if dirn == 1 else step
        if shared is None or nt > 1:
            xbc = _silu(_conv3(main_ref[...], prev_ref[...], next_ref[...], w_ref[...], tpos > 0, tpos < nt - 1)
                        + bias_ref[...])
            x = xbc[:, 0:256]
            bm16, cm16 = xbc[:, 256:512].astype(BF16), xbc[:, 512:768].astype(BF16)
            cbs = [_dot_nt(cm16[:, g * N_D:(g + 1) * N_D], bm16[:, g * N_D:(g + 1) * N_D]) for g in range(G_D)]
            bmt16 = xbc[:, 256:512].T.astype(BF16)
            sm = sm_ref[...]
            lane = lax.broadcasted_iota(jnp.int32, sm.shape, 1)
            dt = _softplus(sm + prm_ref[1:2, 0:128])
            da = jnp.where(jnp.logical_and(lane >= COL_DT, lane < COL_DT + 2 * H_D),
                           dt * (-jnp.exp(prm_ref[0:1, 0:128])), 0.0)
            shared = (x, cm16, cbs, bmt16, dt, da)
        x, cm16, cbs, bmt16, dt, da = shared
        e16 = ce_ref[CE_DT + dirn]
        a_hi, a_lo = _split2(expand(da, e16))
        ac = _dot(cb_ref[dirn], a_hi) + _dot(cb_ref[dirn], a_lo)
        at = _dot(block16, a_hi) + _dot(block16, a_lo)
        ac_t = ac.T
        xdt = x * expand(dt, e16)
        xdt16 = xdt.astype(BF16)
        y = None
        for h in range(H_D):
            seg = jnp.exp((ac[:, h * P_D:h * P_D + 1] - ac_t[h * P_D:h * P_D + 1, :]) + cf_ref[dirn])
            part = _dot((cbs[h * P_D // glanes] * seg).astype(BF16), xdt16 * cb_ref[CB_LANES + h])
            y = part if y is None else y + part
        if dirn == 0:
            y = y + prm_ref[2:3, :] * x
        dirs.append(dict(y=y, eac=jnp.exp(ac), eat=jnp.exp(at), xdd16=(xdt * jnp.exp(at - ac)).astype(BF16),
                         cm16=cm16, bmt16=bmt16, s=s_scr[dirn], rows=[None] * n_chunks))

    for ci in range(n_chunks):
        for dirn, dd in enumerate(dirs):
            c = (n_chunks - 1 - ci) if dirn == 1 else ci
            rows = slice(c * CHUNK, (c + 1) * CHUNK)
            s16 = dd["s"].astype(BF16)
            inter = group_lanes([_dot(dd["cm16"][rows, g * N_D:(g + 1) * N_D], s16) for g in range(G_D)])
            dd["rows"][c] = dd["y"][rows] + inter * dd["eac"][rows]
            upd = group_lanes([_dot(dd["bmt16"][g * N_D:(g + 1) * N_D, :] * cb_ref[CB_LANES + c, 0:N_D, :],
                                    dd["xdd16"]) for g in range(G_D)])
            dd["s"] = dd["s"] * dd["eat"][c * CHUNK:c * CHUNK + 1, :] + upd

    for dirn, (dd, y_ref) in enumerate(zip(dirs, (yf_ref, yb_ref))):
        s_scr[dirn] = dd["s"]
        y_ref[...] = jnp.concatenate(dd["rows"], axis=0)

    @pl.when(step == nt - 1)
    def _fin():
        for d in range(2):
            st = s_scr[d].T
            for h in range(H_D):
                sfin_ref[d, h] = st[h * P_D:(h + 1) * P_D, :]


def _ssd(d_xbc, small, batch, seq_len, conv_w, conv_b, prm, s0):
    nt = seq_len // TILE
    has_state = s0 is not None
    const = lambda b, i: (0, 0)
    const3 = lambda b, i: (0, 0, 0)
    consts = _scan_consts()
    widths = ((768, True), (128, False))
    in_specs = (_tile_specs(batch, seq_len, widths, False) + _tile_specs(batch, seq_len, widths, True)
                + [pl.BlockSpec((3, 768), const), pl.BlockSpec((1, 768), const), pl.BlockSpec((8, 256), const)]
                + [pl.BlockSpec(cst.shape, const3) for cst in consts])
    args = [d_xbc, d_xbc, d_xbc, small] * 2 + [conv_w, conv_b, prm] + list(consts)
    state_spec = pl.BlockSpec((None, 2, H_D, P_D, N_D), lambda b, i: (b, 0, 0, 0, 0))
    if has_state:
        in_specs.append(state_spec)
        args.append(s0)
    return pl.pallas_call(
        functools.partial(_ssd_kernel, nt, has_state),
        grid=(batch, nt),
        in_specs=in_specs,
        out_specs=[pl.BlockSpec((TILE, 256), lambda b, i: (b * nt + i, 0)),
                   pl.BlockSpec((TILE, 256), lambda b, i: (b * nt + nt - 1 - i, 0)), state_spec],
        out_shape=[jax.ShapeDtypeStruct((batch * seq_len, 256), F32)] * 2
        + [jax.ShapeDtypeStruct((batch, 2, H_D, P_D, N_D), F32)],
        scratch_shapes=[pltpu.VMEM((2, N_D, H_D * P_D), F32)],
        compiler_params=_cparams(2),
        name="ssd",
    )(*args)


def _outproj_ffn_kernel(x_ref, mod_ref, oa_ref, obf_ref, obb_ref, bg_ref, oc_ref, ydf_ref, ydb_ref, dz_ref,
                        dng_ref, ssg_ref, wo_ref, l1g_ref, l1b_ref, w1_ref, w2_ref, l2g_ref, l2b_ref, o_ref):
    x = x_ref[...]
    o_b = _rms_groups(obf_ref[...] + obb_ref[...], DV_B) * dng_ref[...] * _silu(bg_ref[...])
    y_d = (ydf_ref[...] + ydb_ref[...]) * _silu(dz_ref[...])
    o_d = _rms_groups(y_d, GROUP_W // G_D) * ssg_ref[...]
    mixed = None
    for idx, part in enumerate((oa_ref[...], o_b, oc_ref[...], o_d)):
        y = _dot(part.astype(BF16), wo_ref[idx * GROUP_W:(idx + 1) * GROUP_W, :])
        mixed = y if mixed is None else mixed + y
    x1 = _layer_norm(ALPHA * x + mod_ref[2:3, :] * mixed, l1g_ref[...], l1b_ref[...])
    h16 = (x1 * (1.0 + mod_ref[4:5, :]) + mod_ref[3:4, :]).astype(BF16)
    ff = None
    fcol = 1024
    for j in range(D_FF // fcol):
        a = jnp.square(jnp.maximum(_dot(h16, w1_ref[:, j * fcol:(j + 1) * fcol]), 0.0)).astype(BF16)
        y = _dot(a, w2_ref[j * fcol:(j + 1) * fcol, :])
        ff = y if ff is None else ff + y
    o_ref[...] = _layer_norm(ALPHA * x1 + mod_ref[5:6, :] * ff, l2g_ref[...], l2b_ref[...])


def _outproj_ffn(x, mods, o_a, o_bf, o_bb, b_gate, o_c, y_df, y_db, d_z, wts, layer, seq_len, per_batch_mod):
    t = x.shape[0]
    tb = min(TB, seq_len)
    nb_seq = seq_len // tb
    mod_idx = (lambda i: (layer, 1 + i // nb_seq, 0, 0)) if per_batch_mod else (lambda i: (layer, 0, 0, 0))
    row = lambda wd: pl.BlockSpec((tb, wd), lambda i: (i, 0))
    const = lambda i: (0, 0)
    single = pl.Buffered(1)
    dn_g, ss_g, w_out16, l1g, l1b, w_ff1_16, w_ff2_16, l2g, l2b = wts
    in_specs = [row(D_MODEL), pl.BlockSpec((None, None, 6, D_MODEL), mod_idx),
                row(256), row(256), row(256), row(256), row(256), row(256), row(256), row(256),
                pl.BlockSpec((1, 256), const), pl.BlockSpec((1, 256), const),
                _layer_spec(w_out16, layer, pipeline_mode=single),
                pl.BlockSpec((1, D_MODEL), const), pl.BlockSpec((1, D_MODEL), const),
                _layer_spec(w_ff1_16, layer, pipeline_mode=single),
                _layer_spec(w_ff2_16, layer, pipeline_mode=single),
                pl.BlockSpec((1, D_MODEL), const), pl.BlockSpec((1, D_MODEL), const)]
    return pl.pallas_call(
        _outproj_ffn_kernel,
        grid=(t // tb,),
        in_specs=in_specs,
        out_specs=row(D_MODEL),
        out_shape=jax.ShapeDtypeStruct((t, D_MODEL), F32),
        compiler_params=_cparams(1),
        name="outproj_ffn",
    )(x, mods, o_a, o_bf, o_bb, b_gate, o_c, y_df, y_db, d_z, *wts)


def _rope_tables(length, dim, width):
    rows = length // GRID_W
    row_pos = jnp.repeat(jnp.arange(rows, dtype=F32), GRID_W)
    col_pos = jnp.tile(jnp.arange(GRID_W, dtype=F32), rows)
    half = dim // 2
    inv_freq = ROPE_BASE ** (-jnp.arange(0, half, 2, dtype=F32) / half)
    ang_r = row_pos[:, None] * inv_freq
    ang_c = col_pos[:, None] * inv_freq
    ang = jnp.concatenate([ang_r, ang_r, ang_c, ang_c], axis=-1)
    cos, sin = jnp.cos(ang), jnp.sin(ang)
    first = (jnp.arange(dim) % (dim // 2)) < (dim // 4)
    sin_lo = jnp.where(first, -sin, 0.0)
    sin_hi = jnp.where(first, 0.0, sin)
    return cos, sin_lo, sin_hi


def _tile_lanes(tabs, reps):
    return tuple(jnp.tile(t, (1, reps)) for t in tabs)


def _pad_lanes(tabs, width):
    cos, lo, hi = tabs
    pad = width - cos.shape[1]
    return (jnp.pad(cos, ((0, 0), (0, pad)), constant_values=1.0), jnp.pad(lo, ((0, 0), (0, pad))),
            jnp.pad(hi, ((0, 0), (0, pad))))


def _permute_w_in(w_in):
    offs = {}
    o = 0
    for name, size in zip(SPLIT_NAMES, SPLIT_SIZES):
        offs[name] = (o, size)
        o += size
    cols = []
    for names, width in zip(OUT_GROUPS, OUT_WIDTHS):
        used = 0
        for n in names:
            s, size = offs[n]
            cols.append(w_in[:, :, s:s + size])
            used += size
        if used < width:
            cols.append(jnp.zeros(w_in.shape[:2] + (width - used,), w_in.dtype))
    return jnp.concatenate(cols, axis=-1).astype(BF16)


def _param_row(vals, col0, width=128, rows=8):
    out = jnp.zeros((DEPTH, rows, width), F32)
    for r, v in enumerate(vals):
        out = out.at[:, r, col0:col0 + v.shape[1]].set(v.astype(F32))
    return out


def kernel(x_prompt, x_sample, cache_diff_k, cache_diff_v, state_delta, cache_mla_ckv, cache_mla_krope, state_ssm, c, c_ctx, w_mod, b_mod, w_in, diff_lam, diff_norm, dn_conv, dn_a_log, dn_dt_bias, dn_norm, mla_q_norm, mla_kv_norm, mla_w_uq, mla_w_uk, mla_w_uv, ssm_conv_w, ssm_conv_b, ssm_a_log, ssm_dt_bias, ssm_d, ssm_norm, w_out, ln1_g, ln1_b, ln2_g, ln2_b, w_ff1, w_ff2):
    nb_p, len_p = x_prompt.shape[:2]
    nb_s, len_s = x_sample.shape[:2]
    past_len = cache_diff_k.shape[2]
    assert past_len == TILE and nb_s + 1 <= 8

    cvec = jnp.concatenate([c_ctx[None, :], c, jnp.zeros((8 - 1 - nb_s, D_MODEL), F32)], axis=0)
    mods = _mod_call(cvec, w_mod, b_mod).reshape(DEPTH, 8, 6, D_MODEL)

    w_in_p = _permute_w_in(w_in)
    w_uq = mla_w_uq.reshape(DEPTH, Q_RANK, H_C, NOPE_C + ROPE_C)
    w_uq_p = jnp.concatenate([w_uq[..., :NOPE_C].reshape(DEPTH, Q_RANK, H_C * NOPE_C),
                              w_uq[..., NOPE_C:].reshape(DEPTH, Q_RANK, H_C * ROPE_C)], axis=-1).astype(BF16)
    w_uk16, w_uv16 = mla_w_uk.astype(BF16), mla_w_uv.astype(BF16)
    w_out16, w_ff1_16, w_ff2_16 = w_out.astype(BF16), w_ff1.astype(BF16), w_ff2.astype(BF16)
    diff_g = jnp.tile(diff_norm, (1, H_A))[:, None, :]
    lam_init = jnp.asarray([0.8 - 0.6 * math.exp(-0.3 * l) for l in range(DEPTH)], F32)
    lam_rows = jnp.concatenate([diff_lam, jnp.broadcast_to(lam_init[:, None, None], (DEPTH, 4, DQK_A))], axis=1)
    dn_g = jnp.tile(dn_norm, (1, H_B))[:, None, :]
    ss_g = ssm_norm[:, None, :]
    dn_prm = _param_row([dn_a_log.reshape(DEPTH, 2 * H_B), dn_dt_bias.reshape(DEPTH, 2 * H_B)], COL_DECAY)
    ssm_prm = _param_row([ssm_a_log.reshape(DEPTH, 2 * H_D), ssm_dt_bias.reshape(DEPTH, 2 * H_D)], COL_DT, width=256)
    ssm_prm = ssm_prm.at[:, 2, :].set(jnp.repeat(ssm_d, P_D, axis=1))

    tabs_a = _tile_lanes(_rope_tables(len_s, DQK_A, 256), 256 // DQK_A)
    tabs_c = _rope_tables(len_s, ROPE_C, 128)
    tabs_small = _pad_lanes(tabs_c, 128)
    tabs_q = _tile_lanes(tabs_c, 128 // ROPE_C)

    past_kr = jnp.pad(cache_mla_krope, ((0, 0), (0, 0), (0, 0), (0, 128 - ROPE_C)))

    def layer(x, l, batch, seq_len, is_latent):
        (a_qkv, b_qkv, b_gate, c_q, d_z, d_xbc, c_kv, small) = _inproj(
            x, mods, w_in_p, l, seq_len, is_latent, tabs_a + tabs_small if is_latent else None)
        if is_latent:
            past_a = (cache_diff_k[:, l].reshape(batch, past_len, 256), cache_diff_v[:, l].reshape(batch, past_len, 256))
            past_c = (cache_mla_ckv[:, l], past_kr[:, l])
            s0_b, s0_d = state_delta[:, l], state_ssm[:, l]
        else:
            past_a = past_c = s0_b = s0_d = None
        o_a = _diff_attn(a_qkv, batch, seq_len, lam_rows[l], diff_g[l], past_a)
        o_c, ckv_n = _mla_attn(c_q, c_kv, small, batch, seq_len,
                               (mla_q_norm[l][None, :], mla_kv_norm[l][None, :]), (w_uq_p, w_uk16, w_uv16), l,
                               past_c, tabs_q if is_latent else None)
        o_bf, o_bb, st_b = _deltanet(b_qkv, small, batch, seq_len, dn_conv[l], dn_prm[l], s0_b)
        y_df, y_db, st_d = _ssd(d_xbc, small, batch, seq_len, ssm_conv_w[l], ssm_conv_b[l][None, :], ssm_prm[l], s0_d)
        x = _outproj_ffn(x, mods, o_a, o_bf, o_bb, b_gate, o_c, y_df, y_db, d_z,
                         (dn_g[l], ss_g[l], w_out16, ln1_g[l][None, :], ln1_b[l][None, :], w_ff1_16, w_ff2_16,
                          ln2_g[l][None, :], ln2_b[l][None, :]), l, seq_len, is_latent)
        return x, (a_qkv, st_b, ckv_n, small, st_d)

    xp = x_prompt.reshape(nb_p * len_p, D_MODEL)
    xs = x_sample.reshape(nb_s * len_s, D_MODEL)
    ctx = ([], [], [], [], [], [])
    for l in range(DEPTH):
        xp, (a_qkv, st_b, ckv_n, small, st_d) = layer(xp, l, nb_p, len_p, False)
        xs, _ = layer(xs, l, nb_s, len_s, True)
        ctx[0].append(a_qkv[:, 256:512].reshape(nb_p, len_p, H_A, 2, DQK_A))
        ctx[1].append(a_qkv[:, 512:768].reshape(nb_p, len_p, H_A, DV_A))
        ctx[2].append(st_b)
        ctx[3].append(ckv_n.reshape(nb_p, len_p, KV_RANK))
        ctx[4].append(small[:, COL_KR:COL_KR + ROPE_C].reshape(nb_p, len_p, ROPE_C))
        ctx[5].append(st_d)
    return (xp.reshape(nb_p, len_p, D_MODEL), xs.reshape(nb_s, len_s, D_MODEL)) + tuple(
        jnp.stack(t, axis=1) for t in ctx)
```
